```python
import math
import jax, jax.numpy as jnp
from jax import lax
import numpy as np

D_MODEL = 1024
BATCH = 8
SEQ = 8192
DEPTH = 1
DEC_BATCH = 128
DEC_SEQ = 4
PAST_LEN = 8192
PAGE_SIZE = 128

N_META = 16
D_MIX = D_MODEL
N_HEADS = 4
HEAD_DIM = 64
V_DIM = 2 * HEAD_DIM
D_ATTN = N_HEADS * V_DIM
D_QK = N_HEADS * 2 * HEAD_DIM
D_POOL = D_MIX - D_ATTN
POOL_WINDOWS = (2, 4, 8, 16)
N_POOL_GROUPS = len(POOL_WINDOWS)
POOL_GROUP = D_POOL // N_POOL_GROUPS
MAX_WINDOW = max(POOL_WINDOWS)
POOL_STATE = MAX_WINDOW - 1
D_IN = 2 * D_QK + D_ATTN + D_POOL
N_EXPERT_GROUPS = 4
EXPERTS_PER_GROUP = 4
N_EXPERTS = N_EXPERT_GROUPS * EXPERTS_PER_GROUP
TOP_K_IN_GROUP = 2
D_EXPERT = 512
ROPE_THETA = 10000.0
EPS = 1e-6
Q_BLOCK = 128
NEG = -1e30

kernel_name = "hymba_diffattn_pool_hmoe_step"


def lambda_init(layer_idx):
    return 0.8 - 0.6 * math.exp(-0.3 * (layer_idx - 1))


def rms_norm(x, g):
    xf = x.astype(jnp.float32)
    y = xf * lax.rsqrt(jnp.mean(xf * xf, axis=-1, keepdims=True) + EPS)
    return (y * g.astype(jnp.float32)).astype(x.dtype)


def rope(x, pos):
    half = HEAD_DIM // 2
    inv_freq = ROPE_THETA ** (-jnp.arange(0, HEAD_DIM, 2, dtype=jnp.float32) / HEAD_DIM)
    ang = pos.astype(jnp.float32)[:, None] * inv_freq[None, :]
    cos = jnp.cos(ang)[None, :, None, None, :]
    sin = jnp.sin(ang)[None, :, None, None, :]
    xf = x.astype(jnp.float32)
    x1, x2 = xf[..., :half], xf[..., half:]
    out = jnp.concatenate([x1 * cos - x2 * sin, x2 * cos + x1 * sin], axis=-1)
    return out.astype(x.dtype)


def project(x, norm_g, w_in, q_norm_g, k_norm_g, pos):
    B, S, _ = x.shape
    h = rms_norm(x, norm_g)
    z = jnp.einsum('bsd,de->bse', h, w_in)
    q, k, v, u = jnp.split(z, [D_QK, 2 * D_QK, 2 * D_QK + D_ATTN], axis=-1)
    q = rope(rms_norm(q.reshape(B, S, N_HEADS, 2, HEAD_DIM), q_norm_g), pos)
    k = rope(rms_norm(k.reshape(B, S, N_HEADS, 2, HEAD_DIM), k_norm_g), pos)
    v = v.reshape(B, S, N_HEADS, V_DIM)
    return q, k, v, u


def diff_lambda(lq1, lk1, lq2, lk2, lam_init):
    f = lambda a: a.astype(jnp.float32)
    return jnp.exp(jnp.sum(f(lq1) * f(lk1))) - jnp.exp(jnp.sum(f(lq2) * f(lk2))) + lam_init


def diff_attn_prompt(q, k, v, lam):
    B, L = q.shape[:2]
    n_blk = -(-L // Q_BLOCK)
    Lp = n_blk * Q_BLOCK
    scale = HEAD_DIM ** -0.5
    qf = jnp.pad(q.astype(jnp.float32) * scale, ((0, 0), (0, Lp - L), (0, 0), (0, 0), (0, 0)))
    qb = qf.reshape(B, n_blk, Q_BLOCK, N_HEADS, 2, HEAD_DIM).transpose(1, 0, 2, 3, 4, 5)
    kf = k.astype(jnp.float32)
    vf = v.astype(jnp.float32)
    key_pos = jnp.arange(L)

    def one_block(args):
        i, qblk = args
        qpos = i * Q_BLOCK + jnp.arange(Q_BLOCK)
        s = jnp.einsum('bqhmd,bkhmd->bmhqk', qblk, kf)
        s = jnp.where(key_pos[None, :] <= qpos[:, None], s, NEG)
        p = jax.nn.softmax(s, axis=-1)
        a = p[:, 0] - lam * p[:, 1]
        return jnp.einsum('bhqk,bkhd->bqhd', a, vf)

    o = lax.map(one_block, (jnp.arange(n_blk), qb))
    o = o.transpose(1, 0, 2, 3, 4).reshape(B, Lp, N_HEADS, V_DIM)[:, :L]
    return o


def online_update(carry, qf, kb, vb, mask):
    m, l, acc = carry
    s = jnp.einsum('bqhmd,bkhmd->bmhqk', qf, kb.astype(jnp.float32))
    if mask is not None:
        s = jnp.where(mask, s, NEG)
    m_new = jnp.maximum(m, jnp.max(s, axis=-1))
    alpha = jnp.exp(m - m_new)
    p = jnp.exp(s - m_new[..., None])
    l = alpha * l + jnp.sum(p, axis=-1)
    acc = alpha[..., None] * acc + jnp.einsum('bmhqk,bkhd->bmhqd', p, vb.astype(jnp.float32))
    return (m_new, l, acc)


def diff_attn_sample(q, k_new, v_new, cache_k, cache_v, layer, page_table, lam):
    Bd, S = q.shape[:2]
    page = cache_k.shape[2]
    qf = q.astype(jnp.float32) * (HEAD_DIM ** -0.5)
    init = (jnp.full((Bd, 2, N_HEADS, S), NEG, jnp.float32),
            jnp.zeros((Bd, 2, N_HEADS, S), jnp.float32),
            jnp.zeros((Bd, 2, N_HEADS, S, V_DIM), jnp.float32))

    def page_step(carry, phys):
        kb = cache_k[layer, phys].reshape(Bd, page, N_HEADS, 2, HEAD_DIM)
        vb = cache_v[layer, phys]
        return online_update(carry, qf, kb, vb, None), None

    carry, _ = lax.scan(page_step, init, page_table.T)
    causal = jnp.arange(S)[None, :] <= jnp.arange(S)[:, None]
    m, l, acc = online_update(carry, qf, k_new, v_new, causal)
    o = acc / l[..., None]
    o = o[:, 0] - lam * o[:, 1]
    return o.transpose(0, 2, 1, 3)


def pool_mix(u_ext, n_out, pool_w, pool_scale):
    B, Le, _ = u_ext.shape
    uf = u_ext.astype(jnp.float32)
    cp = jnp.pad(jnp.cumsum(uf, axis=1), ((0, 0), (MAX_WINDOW, 0), (0, 0)))
    t = jnp.arange(Le - n_out, Le)
    lo, hi = Le - n_out + MAX_WINDOW, Le + MAX_WINDOW
    outs = []
    for g, w in enumerate(POOL_WINDOWS):
        cs = slice(g * POOL_GROUP, (g + 1) * POOL_GROUP)
        win = cp[:, lo:hi, cs] - cp[:, lo - w:hi - w, cs]
        cnt = jnp.minimum(w, t + 1).astype(jnp.float32)[None, :, None]
        d = win / cnt - uf[:, Le - n_out:, cs]
        outs.append(jnp.einsum('bsc,cd->bsd', d, pool_w[g].astype(jnp.float32)))
    o = jnp.concatenate(outs, axis=-1) * pool_scale.astype(jnp.float32)
    return o.astype(u_ext.dtype)


def merge_out(o_att, o_pool, subln_g, lam_init, w_out):
    B, S = o_att.shape[:2]
    a = (rms_norm(o_att, subln_g).astype(jnp.float32) * (1.0 - lam_init)).reshape(B, S, D_ATTN)
    cat = jnp.concatenate([a.astype(o_pool.dtype), o_pool], axis=-1)
    return jnp.einsum('bsc,cd->bsd', cat, w_out)


def hier_moe(h, rgw, rgb, rew, reb, wg, wu, wd):
    B, S, _ = h.shape
    gl = (jnp.einsum('bsd,dg->bsg', h, rgw) + rgb).astype(jnp.float32)
    g_idx = jnp.argmax(gl, axis=-1)
    g_prob = jnp.max(jax.nn.softmax(gl, axis=-1), axis=-1)
    el = (jnp.einsum('bsd,de->bse', h, rew) + reb).astype(jnp.float32)
    el = el.reshape(B, S, N_EXPERT_GROUPS, EXPERTS_PER_GROUP)
    el_sel = jnp.take_along_axis(el, g_idx[..., None, None], axis=2)[..., 0, :]
    top_v, top_i = lax.top_k(el_sel, TOP_K_IN_GROUP)
    top_w = jax.nn.softmax(top_v, axis=-1) * g_prob[..., None]
    expert_id = g_idx[..., None] * EXPERTS_PER_GROUP + top_i
    combine = jnp.sum(jax.nn.one_hot(expert_id, N_EXPERTS, dtype=jnp.float32) * top_w[..., None], axis=-2)
    combine = combine.astype(h.dtype)
    y = jnp.zeros_like(h)
    for e in range(N_EXPERTS):
        a = jax.nn.silu(jnp.einsum('bsd,df->bsf', h, wg[e])) * jnp.einsum('bsd,df->bsf', h, wu[e])
        y = y + combine[..., e:e + 1] * jnp.einsum('bsf,fd->bsd', a, wd[e])
    return y


def setup_inputs(seed: int = 0) -> dict:
    key = jax.random.key(seed)
    ks = jax.random.split(key, 32)
    n_pages = PAST_LEN // PAGE_SIZE
    n_used = DEC_BATCH * n_pages
    n_phys = (n_used * 5) // 4
    nrm = lambda k, shape, s=1.0: jax.random.normal(k, shape, jnp.float32) * s
    page_table = jax.random.permutation(ks[5], n_phys)[:n_used].reshape(DEC_BATCH, n_pages).astype(jnp.int32)
    return {
        "x_prompt": nrm(ks[0], (BATCH, SEQ, D_MODEL)),
        "x_sample": nrm(ks[1], (DEC_BATCH, DEC_SEQ, D_MODEL)),
        "cache_k": nrm(ks[2], (DEPTH, n_phys, PAGE_SIZE, N_HEADS, 2 * HEAD_DIM)),
        "cache_v": nrm(ks[3], (DEPTH, n_phys, PAGE_SIZE, N_HEADS, V_DIM)),
        "state_pool": nrm(ks[4], (DEPTH, DEC_BATCH, POOL_STATE, D_POOL)),
        "page_table": page_table,
        "meta_tokens": nrm(ks[6], (N_META, D_MODEL)),
        "norm_attn_g": 1.0 + nrm(ks[7], (DEPTH, D_MODEL), 0.01),
        "w_in": nrm(ks[8], (DEPTH, D_MODEL, D_IN), D_MODEL ** -0.5),
        "q_norm_g": 1.0 + nrm(ks[9], (DEPTH, HEAD_DIM), 0.01),
        "k_norm_g": 1.0 + nrm(ks[10], (DEPTH, HEAD_DIM), 0.01),
        "lambda_q1": nrm(ks[11], (DEPTH, HEAD_DIM), 0.1),
        "lambda_k1": nrm(ks[12], (DEPTH, HEAD_DIM), 0.1),
        "lambda_q2": nrm(ks[13], (DEPTH, HEAD_DIM), 0.1),
        "lambda_k2": nrm(ks[14], (DEPTH, HEAD_DIM), 0.1),
        "subln_g": 1.0 + nrm(ks[15], (DEPTH, V_DIM), 0.01),
        "pool_w": nrm(ks[16], (DEPTH, N_POOL_GROUPS, POOL_GROUP, POOL_GROUP), POOL_GROUP ** -0.5),
        "pool_scale": 1.0 + nrm(ks[17], (DEPTH, D_POOL), 0.1),
        "w_out": nrm(ks[18], (DEPTH, D_MIX, D_MODEL), D_MIX ** -0.5),
        "norm_ffn_g": 1.0 + nrm(ks[19], (DEPTH, D_MODEL), 0.01),
        "router_group_w": nrm(ks[20], (DEPTH, D_MODEL, N_EXPERT_GROUPS), D_MODEL ** -0.5),
        "router_group_b": nrm(ks[21], (DEPTH, N_EXPERT_GROUPS), 0.01),
        "router_expert_w": nrm(ks[22], (DEPTH, D_MODEL, N_EXPERTS), D_MODEL ** -0.5),
        "router_expert_b": nrm(ks[23], (DEPTH, N_EXPERTS), 0.01),
        "expert_w_gate": nrm(ks[24], (DEPTH, N_EXPERTS, D_MODEL, D_EXPERT), D_MODEL ** -0.5),
        "expert_w_up": nrm(ks[25], (DEPTH, N_EXPERTS, D_MODEL, D_EXPERT), D_MODEL ** -0.5),
        "expert_w_down": nrm(ks[26], (DEPTH, N_EXPERTS, D_EXPERT, D_MODEL), D_EXPERT ** -0.5),
    }


def reference(x_prompt, x_sample, cache_k, cache_v, state_pool, page_table, meta_tokens,
              norm_attn_g, w_in, q_norm_g, k_norm_g, lambda_q1, lambda_k1, lambda_q2, lambda_k2,
              subln_g, pool_w, pool_scale, w_out, norm_ffn_g, router_group_w, router_group_b,
              router_expert_w, router_expert_b, expert_w_gate, expert_w_up, expert_w_down):
    B = x_prompt.shape[0]
    Bd, S_dec = x_sample.shape[:2]
    past_len = page_table.shape[1] * cache_k.shape[2]
    meta = jnp.broadcast_to(meta_tokens.astype(x_prompt.dtype)[None], (B, N_META, D_MODEL))
    xp = jnp.concatenate([meta, x_prompt], axis=1)
    xs = x_sample
    L = xp.shape[1]
    pos_p = jnp.arange(L)
    pos_s = past_len + jnp.arange(S_dec)
    kp_l, vp_l, pp_l, ks_l, vs_l, ps_l = [], [], [], [], [], []
    for l in range(DEPTH):
        lam_init = lambda_init(l + 1)
        lam = diff_lambda(lambda_q1[l], lambda_k1[l], lambda_q2[l], lambda_k2[l], lam_init)
        q, k, v, u = project(xp, norm_attn_g[l], w_in[l], q_norm_g[l], k_norm_g[l], pos_p)
        o_att = diff_attn_prompt(q, k, v, lam)
        o_pool = pool_mix(u, L, pool_w[l], pool_scale[l])
        xp = xp + merge_out(o_att, o_pool, subln_g[l], lam_init, w_out[l])
        xp = xp + hier_moe(rms_norm(xp, norm_ffn_g[l]), router_group_w[l], router_group_b[l],
                           router_expert_w[l], router_expert_b[l],
                           expert_w_gate[l], expert_w_up[l], expert_w_down[l])
        kp_l.append(k.reshape(B, L, N_HEADS, 2 * HEAD_DIM))
        vp_l.append(v)
        pp_l.append(u[:, L - POOL_STATE:])
        q, k, v, u = project(xs, norm_attn_g[l], w_in[l], q_norm_g[l], k_norm_g[l], pos_s)
        o_att = diff_attn_sample(q, k, v, cache_k, cache_v, l, page_table, lam)
        u_ext = jnp.concatenate([state_pool[l].astype(u.dtype), u], axis=1)
        o_pool = pool_mix(u_ext, S_dec, pool_w[l], pool_scale[l])
        xs = xs + merge_out(o_att, o_pool, subln_g[l], lam_init, w_out[l])
        xs = xs + hier_moe(rms_norm(xs, norm_ffn_g[l]), router_group_w[l], router_group_b[l],
                           router_expert_w[l], router_expert_b[l],
                           expert_w_gate[l], expert_w_up[l], expert_w_down[l])
        ks_l.append(k.reshape(Bd, S_dec, N_HEADS, 2 * HEAD_DIM))
        vs_l.append(v)
        ps_l.append(u_ext[:, u_ext.shape[1] - POOL_STATE:])
    y_prompt = xp[:, N_META:]
    y_sample = xs
    new_k_prompt = jnp.stack(kp_l)
    new_v_prompt = jnp.stack(vp_l)
    new_pool_prompt = jnp.stack(pp_l)
    new_k_sample = jnp.stack(ks_l)
    new_v_sample = jnp.stack(vs_l)
    new_pool_sample = jnp.stack(ps_l)
    return (y_prompt, y_sample, new_k_prompt, new_v_prompt, new_pool_prompt, new_k_sample, new_v_sample, new_pool_sample)
```

```python
import functools
import math

import jax
import jax.numpy as jnp
from jax import lax
from jax.experimental import pallas as pl
from jax.experimental.pallas import tpu as pltpu

N_META = 16
N_HEADS = 4
HEAD_DIM = 64
V_DIM = 2 * HEAD_DIM
D_QK = N_HEADS * 2 * HEAD_DIM
D_ATTN = N_HEADS * V_DIM
POOL_WINDOWS = (2, 4, 8, 16)
POOL_GROUP = 128
MAX_WINDOW = max(POOL_WINDOWS)
POOL_STATE = MAX_WINDOW - 1
D_POOL = POOL_GROUP * len(POOL_WINDOWS)
N_EXPERT_GROUPS = 4
EXPERTS_PER_GROUP = 4
N_EXPERTS = N_EXPERT_GROUPS * EXPERTS_PER_GROUP
ROPE_THETA = 10000.0
EPS = 1e-6
NEG = -1e30
LOG2E = 1.4426950408889634
LAM_INIT = 0.8 - 0.6 * math.exp(-0.3 * 0)
Q_SCALE = HEAD_DIM ** -0.5 * LOG2E

LANES = 128
ROUTER_LANES = LANES
VMEM_LIMIT = 48 * 1024 * 1024

PROJ_ROWS = 512
ATTN_ROWS = 512
MOE_ROWS = 512
DECODE_PAGES = 8

F32 = jnp.float32
BF16 = jnp.bfloat16


def _dot(a, b):
    return jnp.dot(a, b, preferred_element_type=F32)


def _dot_nt(a, b):
    return lax.dot_general(a, b, (((1,), (1,)), ((), ())), preferred_element_type=F32)


def _rms(x, g):
    return x * lax.rsqrt(jnp.mean(x * x, axis=-1, keepdims=True) + EPS) * g


def _qk_norm_rope(z, gmat, g_t, cos_t, sin_t):
    ms = _dot((z * z).astype(BF16), gmat)
    xn = z * lax.rsqrt(ms + EPS) * g_t
    outs = []
    for j in range(z.shape[1] // LANES):
        xb = xn[:, j * LANES:(j + 1) * LANES]
        lane = lax.broadcasted_iota(jnp.int32, xb.shape, 1)
        first_half = (lane % HEAD_DIM) < (HEAD_DIM // 2)
        rot = jnp.where(first_half, pltpu.roll(xb, LANES - HEAD_DIM // 2, 1), pltpu.roll(xb, HEAD_DIM // 2, 1))
        outs.append(xb * cos_t + rot * sin_t)
    return jnp.concatenate(outs, axis=1)


def _project(x, ng, win_ref, gmat, qg, kg, cos_t, sin_t):
    h = _rms(x, ng).astype(BF16)
    q = _qk_norm_rope(_dot(h, win_ref[:, 0:D_QK]), gmat, qg, cos_t, sin_t) * Q_SCALE
    k = _qk_norm_rope(_dot(h, win_ref[:, D_QK:2 * D_QK]), gmat, kg, cos_t, sin_t)
    v = _dot(h, win_ref[:, 2 * D_QK:2 * D_QK + D_ATTN])
    u = _dot(h, win_ref[:, 2 * D_QK + D_ATTN:])
    return q, k, v, u


def _proj_small_kernel(x_ref, cos_ref, sin_ref, st_ref, ng_ref, win_ref, gmat_ref, qg_ref, kg_ref, pw_ref, ps_ref,
                       q_ref, k_ref, v_ref, u_ref, op_ref, *, n_dec, s_dec):
    q, k, v, u = _project(x_ref[...], ng_ref[...], win_ref, gmat_ref[...], qg_ref[...], kg_ref[...],
                          cos_ref[...], sin_ref[...])
    q_ref[...] = q
    k_ref[...] = k
    v_ref[...] = v
    u_ref[...] = u
    for g, w in enumerate(POOL_WINDOWS):
        cs = slice(g * POOL_GROUP, (g + 1) * POOL_GROUP)
        us = [u[s * n_dec:(s + 1) * n_dec, cs] for s in range(s_dec)]
        need = {w - 1 - s for s in range(s_dec) if w - 1 - s > 0}
        suffix = {0: jnp.zeros_like(us[0])}
        run = jnp.zeros_like(us[0])
        for c in range(1, max(need) + 1 if need else 1):
            run = run + st_ref[POOL_STATE - c, :, cs]
            if c in need:
                suffix[c] = run
        for s in range(s_dec):
            win = suffix[max(w - 1 - s, 0)]
            for sp in range(max(0, s - w + 1), s + 1):
                win = win + us[sp]
            d = win * (1.0 / w) - us[s]
            o = _dot(d.astype(BF16), pw_ref[g]) * ps_ref[:, cs]
            op_ref[s * n_dec:(s + 1) * n_dec, cs] = o.astype(BF16)


def _proj_prompt_kernel(x_ref, cos_ref, sin_ref, um_ref, ng_ref, win_ref, gmat_ref, qg_ref, kg_ref, pw_ref, ps_ref,
                        q_ref, kb_ref, vb_ref, kf_ref, vf_ref, op_ref, ut_ref, ubuf):
    i = pl.program_id(1)
    t = x_ref.shape[0]
    q, k, v, u = _project(x_ref[...], ng_ref[...], win_ref, gmat_ref[...], qg_ref[...], kg_ref[...],
                          cos_ref[...], sin_ref[...])
    q_ref[...] = q.astype(BF16)
    kf_ref[...] = k
    kb_ref[...] = k.astype(BF16)
    vf_ref[...] = v
    vb_ref[...] = v.astype(BF16)

    @pl.when(i == 0)
    def _():
        ubuf[0:MAX_WINDOW, :] = um_ref[...]

    ubuf[MAX_WINDOW:MAX_WINDOW + t, :] = u
    for g, w in enumerate(POOL_WINDOWS):
        cs = slice(g * POOL_GROUP, (g + 1) * POOL_GROUP)
        e = ubuf[:, cs]
        a = e + pltpu.roll(e, 1, 0)
        sh = 2
        while sh < w:
            a = a + pltpu.roll(a, sh, 0)
            sh *= 2
        d = a[MAX_WINDOW:, :] * (1.0 / w) - e[MAX_WINDOW:, :]
        o = _dot(d.astype(BF16), pw_ref[g]) * ps_ref[:, cs]
        op_ref[:, cs] = o.astype(BF16)
    tail = ubuf[t:t + MAX_WINDOW, :]
    ut_ref[...] = tail
    ubuf[0:MAX_WINDOW, :] = tail


def _diff_lambda(lq1_ref, lk1_ref, lq2_ref, lk2_ref):
    a = jnp.sum(lq1_ref[...] * lk1_ref[...], axis=-1, keepdims=True)
    b = jnp.sum(lq2_ref[...] * lk2_ref[...], axis=-1, keepdims=True)
    return jnp.exp(a) - jnp.exp(b) + LAM_INIT


def _online_update(s, v, m_s, l_s, acc_s):
    m_old = m_s[...]
    m_new = jnp.maximum(m_old, jnp.max(s, axis=-1, keepdims=True))
    alpha = jnp.exp2(m_old - m_new)
    p = jnp.exp2(s - m_new)
    l_s[...] = alpha * l_s[...] + jnp.sum(p, axis=-1, keepdims=True)
    acc_s[...] = alpha * acc_s[...] + _dot(p.astype(BF16), v)
    m_s[...] = m_new


def _diff_finalize(n, lam, sg, l_s, acc_s):
    o = acc_s[0:n, :] / l_s[0:n, :] - lam * (acc_s[n:2 * n, :] / l_s[n:2 * n, :])
    return _rms(o, sg) * (1.0 - LAM_INIT)


def _attn_prompt_kernel(q_ref, k_ref, v_ref, km_ref, vm_ref, lq1_ref, lk1_ref, lq2_ref, lk2_ref, sg_ref,
                        o_ref, qs, m_s, l_s, acc_s):
    qi = pl.program_id(2)
    tq = q_ref.shape[0]
    q = q_ref[...]
    lane = lax.broadcasted_iota(jnp.int32, q.shape, 1)
    zero = jnp.zeros_like(q)
    qs[0:tq, :] = jnp.where(lane < HEAD_DIM, q, zero)
    qs[tq:2 * tq, :] = jnp.where(lane >= HEAD_DIM, q, zero)

    s = _dot_nt(qs[...], km_ref[...])
    col = lax.broadcasted_iota(jnp.int32, s.shape, 1)
    s = jnp.where(col < N_META, s, NEG)
    m = jnp.max(s, axis=-1, keepdims=True)
    p = jnp.exp2(s - m)
    m_s[...] = m
    l_s[...] = jnp.sum(p, axis=-1, keepdims=True)
    acc_s[...] = _dot(p.astype(BF16), vm_ref[...])

    def chunk(kk, masked):
        start = pl.multiple_of(kk * tq, tq)
        s = _dot_nt(qs[...], k_ref[pl.ds(start, tq), :])
        if masked:
            row = lax.broadcasted_iota(jnp.int32, s.shape, 0)
            colk = lax.broadcasted_iota(jnp.int32, s.shape, 1)
            tok = jnp.where(row >= tq, row - tq, row)
            s = jnp.where(colk <= tok, s, NEG)
        _online_update(s, v_ref[pl.ds(start, tq), :], m_s, l_s, acc_s)

    def body(kk, carry):
        chunk(kk, False)
        return carry

    lax.fori_loop(0, qi, body, 0)
    chunk(qi, True)
    lam = _diff_lambda(lq1_ref, lk1_ref, lq2_ref, lk2_ref)
    o_ref[...] = _diff_finalize(tq, lam, sg_ref[...], l_s, acc_s).astype(BF16)


def _decode_kernel(pt_ref, q_ref, *rest, n_pg, s_dec):
    k_refs = rest[:n_pg]
    v_refs = rest[n_pg:2 * n_pg]
    (kn_ref, vn_ref, lq1_ref, lk1_ref, lq2_ref, lk2_ref, sg_ref, o_ref, m_s, l_s, acc_s) = rest[2 * n_pg:]
    del pt_ref
    g = pl.program_id(1)
    n_rows = q_ref.shape[0]
    half = n_rows // 2

    @pl.when(g == 0)
    def _():
        m_s[...] = jnp.full(m_s.shape, NEG, F32)
        l_s[...] = jnp.zeros(l_s.shape, F32)
        acc_s[...] = jnp.zeros(acc_s.shape, F32)

    q = q_ref[...]
    page_cols = k_refs[0].shape[0]
    row = lax.broadcasted_iota(jnp.int32, (n_rows, page_cols), 0)
    col = lax.broadcasted_iota(jnp.int32, (n_rows, page_cols), 1)
    head_ok = (col % N_HEADS) == ((row % half) // s_dec)
    ss = []
    for j in range(n_pg):
        s = _dot_nt(q, k_refs[j][...].astype(BF16))
        ss.append(jnp.where(head_ok, s, NEG))
    m_old = m_s[...]
    m_new = m_old
    for s in ss:
        m_new = jnp.maximum(m_new, jnp.max(s, axis=-1, keepdims=True))
    alpha = jnp.exp2(m_old - m_new)
    l = alpha * l_s[...]
    acc = alpha * acc_s[...]
    for j in range(n_pg):
        p = jnp.exp2(ss[j] - m_new)
        l = l + jnp.sum(p, axis=-1, keepdims=True)
        acc = acc + _dot(p.astype(BF16), v_refs[j][...].astype(BF16))
    m_s[...] = m_new
    l_s[...] = l
    acc_s[...] = acc

    @pl.when(g == pl.num_programs(1) - 1)
    def _():
        s = _dot_nt(q, kn_ref[...])
        r2 = lax.broadcasted_iota(jnp.int32, s.shape, 0)
        c2 = lax.broadcasted_iota(jnp.int32, s.shape, 1)
        ok = (c2 < s_dec * N_HEADS) & ((c2 % N_HEADS) == ((r2 % half) // s_dec)) & ((c2 // N_HEADS) <= (r2 % s_dec))
        _online_update(jnp.where(ok, s, NEG), vn_ref[...], m_s, l_s, acc_s)
        lam = _diff_lambda(lq1_ref, lk1_ref, lq2_ref, lk2_ref)
        o_ref[...] = _diff_finalize(half, lam, sg_ref[...], l_s, acc_s).astype(BF16)


def _route(logits):
    ninf = float("-inf")
    lane = lax.broadcasted_iota(jnp.int32, logits.shape, 1)
    lane_f = lane.astype(F32)
    big = float(ROUTER_LANES)
    is_g = lane < N_EXPERT_GROUPS
    gl = jnp.where(is_g, logits, ninf)
    gmax = jnp.max(gl, axis=-1, keepdims=True)
    gidx = jnp.min(jnp.where(gl == gmax, lane_f, big), axis=-1, keepdims=True)
    gsum = jnp.sum(jnp.where(is_g, jnp.exp(logits - gmax), 0.0), axis=-1, keepdims=True)
    gprob = 1.0 / gsum
    egrp = ((lane - N_EXPERT_GROUPS) // EXPERTS_PER_GROUP).astype(F32)
    is_e = (lane >= N_EXPERT_GROUPS) & (lane < N_EXPERT_GROUPS + N_EXPERTS) & (egrp == gidx)
    ev = jnp.where(is_e, logits, ninf)
    t1 = jnp.max(ev, axis=-1, keepdims=True)
    i1 = jnp.min(jnp.where(ev == t1, lane_f, big), axis=-1, keepdims=True)
    ev2 = jnp.where(lane_f == i1, ninf, ev)
    t2 = jnp.max(ev2, axis=-1, keepdims=True)
    i2 = jnp.min(jnp.where(ev2 == t2, lane_f, big), axis=-1, keepdims=True)
    r = jnp.exp(t2 - t1)
    w1 = gprob / (1.0 + r)
    w2 = w1 * r
    return jnp.where(lane_f == i1, w1, 0.0) + jnp.where(lane_f == i2, w2, 0.0)


def _moe_kernel(x_ref, oa_ref, op_ref, wo_ref, ng_ref, rw_ref, rb_ref, wg_ref, wu_ref, wd_ref,
                y_ref, h2_s, comb_s, acc_s):
    e = pl.program_id(1)

    @pl.when(e == 0)
    def _():
        x1 = x_ref[...] + _dot(oa_ref[...], wo_ref[0:D_ATTN, :]) + _dot(op_ref[...], wo_ref[D_ATTN:, :])
        h2 = _rms(x1, ng_ref[...])
        h2_s[...] = h2.astype(BF16)
        logits = jnp.dot(h2, rw_ref[...], preferred_element_type=F32, precision=lax.Precision.HIGHEST) + rb_ref[...]
        comb_s[...] = _route(logits)
        acc_s[...] = x1

    hb = h2_s[...]
    gate = _dot(hb, wg_ref[...])
    a = gate * (1.0 / (1.0 + jnp.exp(-gate))) * _dot(hb, wu_ref[...])
    lane = lax.broadcasted_iota(jnp.int32, comb_s.shape, 1)
    c = jnp.sum(jnp.where(lane == e + N_EXPERT_GROUPS, comb_s[...], 0.0), axis=-1, keepdims=True)
    acc_s[...] += _dot((a * c).astype(BF16), wd_ref[...])

    @pl.when(e == pl.num_programs(1) - 1)
    def _():
        y_ref[...] = acc_s[...]


def _params(*sem):
    return pltpu.CompilerParams(dimension_semantics=sem, vmem_limit_bytes=VMEM_LIMIT)


def _full(shape):
    return pl.BlockSpec(shape, lambda *_: (0,) * len(shape))


def _rope_tables(pos):
    inv_freq = ROPE_THETA ** (-jnp.arange(0, HEAD_DIM, 2, dtype=F32) / HEAD_DIM)
    ang = pos.astype(F32)[:, None] * inv_freq[None, :]
    cos, sin = jnp.cos(ang), jnp.sin(ang)
    reps = LANES // HEAD_DIM
    return jnp.tile(jnp.concatenate([cos, cos], 1), (1, reps)), jnp.tile(jnp.concatenate([-sin, sin], 1), (1, reps))


def _moe_call(x, oa, op, wo, ng, rw, rb, wg, wu, wd, rows):
    n, d = x.shape
    f = wg.shape[-1]
    return pl.pallas_call(
        _moe_kernel,
        grid=(n // rows, N_EXPERTS),
        in_specs=[
            pl.BlockSpec((rows, d), lambda i, e: (i, 0)),
            pl.BlockSpec((rows, D_ATTN), lambda i, e: (i, 0)),
            pl.BlockSpec((rows, D_POOL), lambda i, e: (i, 0)),
            _full(wo.shape), _full(ng.shape), _full(rw.shape), _full(rb.shape),
            pl.BlockSpec((None, d, f), lambda i, e: (e, 0, 0)),
            pl.BlockSpec((None, d, f), lambda i, e: (e, 0, 0)),
            pl.BlockSpec((None, f, d), lambda i, e: (e, 0, 0)),
        ],
        out_specs=pl.BlockSpec((rows, d), lambda i, e: (i, 0)),
        out_shape=jax.ShapeDtypeStruct((n, d), F32),
        scratch_shapes=[pltpu.VMEM((rows, d), BF16), pltpu.VMEM((rows, ROUTER_LANES), F32), pltpu.VMEM((rows, d), F32)],
        compiler_params=_params("parallel", "arbitrary"),
        name="merge_moe",
    )(x, oa, op, wo, ng, rw, rb, wg, wu, wd)


def kernel(x_prompt, x_sample, cache_k, cache_v, state_pool, page_table, meta_tokens, norm_attn_g, w_in, q_norm_g, k_norm_g, lambda_q1, lambda_k1, lambda_q2, lambda_k2, subln_g, pool_w, pool_scale, w_out, norm_ffn_g, router_group_w, router_group_b, router_expert_w, router_expert_b, expert_w_gate, expert_w_up, expert_w_down):
    B, SEQ, D = x_prompt.shape
    Bd, S_dec, _ = x_sample.shape
    n_phys, page = cache_k.shape[1], cache_k.shape[2]
    n_pages = page_table.shape[1]
    past_len = n_pages * page
    assert cache_k.shape[0] == 1 and N_META == MAX_WINDOW

    ng = norm_attn_g[0][None, :]
    win = w_in[0].astype(BF16)
    reps = D_QK // HEAD_DIM
    qg = jnp.tile(q_norm_g[0], reps)[None, :]
    kg = jnp.tile(k_norm_g[0], reps)[None, :]
    grp = jnp.arange(D_QK) // HEAD_DIM
    gmat = jnp.where(grp[:, None] == grp[None, :], 1.0 / HEAD_DIM, 0.0).astype(BF16)
    pw = pool_w[0].astype(BF16)
    ps = pool_scale[0][None, :]
    lq1, lk1, lq2, lk2 = (a[0][None, :] for a in (lambda_q1, lambda_k1, lambda_q2, lambda_k2))
    sg = subln_g[0][None, :]
    wo = w_out[0].astype(BF16)
    nfg = norm_ffn_g[0][None, :]
    pad = ROUTER_LANES - N_EXPERT_GROUPS - N_EXPERTS
    rw = jnp.concatenate([router_group_w[0], router_expert_w[0], jnp.zeros((D, pad), F32)], axis=1)
    rb = jnp.concatenate([router_group_b[0], router_expert_b[0], jnp.zeros((pad,), F32)])[None, :]
    wg = expert_w_gate[0].astype(BF16)
    wu = expert_w_up[0].astype(BF16)
    wd = expert_w_down[0].astype(BF16)

    n_s = Bd * S_dec
    n_small = n_s + N_META
    xs_t = jnp.transpose(x_sample, (1, 0, 2)).reshape(n_s, D)
    x_small = jnp.concatenate([xs_t, meta_tokens.astype(F32)], axis=0)
    pos_small = jnp.concatenate([jnp.repeat(past_len + jnp.arange(S_dec), Bd), jnp.arange(N_META)])
    cos_s, sin_s = _rope_tables(pos_small)
    st_t = jnp.transpose(state_pool[0], (1, 0, 2))
    q_s, k_s, v_s, u_s, opool_s = pl.pallas_call(
        functools.partial(_proj_small_kernel, n_dec=Bd, s_dec=S_dec),
        out_shape=[jax.ShapeDtypeStruct((n_small, D_QK), F32)] * 4 + [jax.ShapeDtypeStruct((n_s, D_POOL), BF16)],
        compiler_params=pltpu.CompilerParams(vmem_limit_bytes=VMEM_LIMIT),
        name="proj_small",
    )(x_small, cos_s, sin_s, st_t, ng, win, gmat, qg, kg, pw, ps)
    k_meta, v_meta, u_meta = k_s[n_s:], v_s[n_s:], u_s[n_s:]

    tp = min(PROJ_ROWS, SEQ)
    cos_p, sin_p = _rope_tables(N_META + jnp.arange(SEQ))
    row_spec = lambda w: pl.BlockSpec((None, tp, w), lambda b, i: (b, i, 0))
    tab_spec = pl.BlockSpec((tp, LANES), lambda b, i: (i, 0))
    q_p, kb_p, vb_p, kf_p, vf_p, opool_p, utail_p = pl.pallas_call(
        _proj_prompt_kernel,
        grid=(B, SEQ // tp),
        in_specs=[row_spec(D), tab_spec, tab_spec, _full(u_meta.shape), _full(ng.shape), _full(win.shape),
                  _full(gmat.shape), _full(qg.shape), _full(kg.shape), _full(pw.shape), _full(ps.shape)],
        out_specs=[row_spec(D_QK)] * 6 + [pl.BlockSpec((None, MAX_WINDOW, D_POOL), lambda b, i: (b, 0, 0))],
        out_shape=[jax.ShapeDtypeStruct((B, SEQ, D_QK), BF16)] * 3 + [jax.ShapeDtypeStruct((B, SEQ, D_QK), F32)] * 2
        + [jax.ShapeDtypeStruct((B, SEQ, D_POOL), BF16), jax.ShapeDtypeStruct((B, MAX_WINDOW, D_POOL), F32)],
        scratch_shapes=[pltpu.VMEM((MAX_WINDOW + tp, D_POOL), F32)],
        compiler_params=_params("parallel", "arbitrary"),
        name="proj_prompt",
    )(x_prompt, cos_p, sin_p, u_meta, ng, win, gmat, qg, kg, pw, ps)

    tq = min(ATTN_ROWS, SEQ)
    km_pad = jnp.zeros((LANES, D_QK), BF16).at[:N_META].set(k_meta.astype(BF16))
    vm_pad = jnp.zeros((LANES, D_ATTN), BF16).at[:N_META].set(v_meta.astype(BF16))
    vec_spec3 = lambda w: pl.BlockSpec((1, w), lambda b, h, i: (0, 0))
    oatt_p = pl.pallas_call(
        _attn_prompt_kernel,
        grid=(B, N_HEADS, SEQ // tq),
        in_specs=[
            pl.BlockSpec((None, tq, V_DIM), lambda b, h, i: (b, i, h)),
            pl.BlockSpec((None, SEQ, V_DIM), lambda b, h, i: (b, 0, h)),
            pl.BlockSpec((None, SEQ, V_DIM), lambda b, h, i: (b, 0, h)),
            pl.BlockSpec((LANES, V_DIM), lambda b, h, i: (0, h)),
            pl.BlockSpec((LANES, V_DIM), lambda b, h, i: (0, h)),
            vec_spec3(HEAD_DIM), vec_spec3(HEAD_DIM), vec_spec3(HEAD_DIM), vec_spec3(HEAD_DIM), vec_spec3(V_DIM),
        ],
        out_specs=pl.BlockSpec((None, tq, V_DIM), lambda b, h, i: (b, i, h)),
        out_shape=jax.ShapeDtypeStruct((B, SEQ, D_ATTN), BF16),
        scratch_shapes=[pltpu.VMEM((2 * tq, V_DIM), BF16), pltpu.VMEM((2 * tq, 1), F32), pltpu.VMEM((2 * tq, 1), F32),
                        pltpu.VMEM((2 * tq, V_DIM), F32)],
        compiler_params=_params("parallel", "parallel", "arbitrary"),
        name="attn_prompt",
    )(q_p, kb_p, vb_p, km_pad, vm_pad, lq1, lk1, lq2, lk2, sg)

    hs = N_HEADS * S_dec
    qd = q_s[:n_s].reshape(S_dec, Bd, N_HEADS, 2, HEAD_DIM)
    qd = jnp.transpose(qd, (1, 3, 2, 0, 4))
    qmat = jnp.zeros((Bd, 2, hs, 2, HEAD_DIM), F32)
    qmat = qmat.at[:, 0, :, 0].set(qd[:, 0].reshape(Bd, hs, HEAD_DIM)).at[:, 1, :, 1].set(qd[:, 1].reshape(Bd, hs, HEAD_DIM))
    qmat = qmat.reshape(Bd, 2 * hs, V_DIM).astype(BF16)
    new_rows = lambda a: jnp.transpose(a[:n_s].reshape(S_dec, Bd, hs // S_dec * V_DIM), (1, 0, 2)).reshape(Bd, hs, V_DIM)
    kn_pad = jnp.zeros((Bd, LANES, V_DIM), BF16).at[:, :hs].set(new_rows(k_s).astype(BF16))
    vn_pad = jnp.zeros((Bd, LANES, V_DIM), BF16).at[:, :hs].set(new_rows(v_s).astype(BF16))
    ck = cache_k.reshape(n_phys, page * N_HEADS, V_DIM)
    cv = cache_v.reshape(n_phys, page * N_HEADS, V_DIM)
    n_pg = math.gcd(DECODE_PAGES, n_pages)
    page_spec = lambda j: pl.BlockSpec((None, page * N_HEADS, V_DIM), lambda b, g, pt: (pt[b, g * n_pg + j], 0, 0))
    vec_spec2 = lambda w: pl.BlockSpec((1, w), lambda b, g, pt: (0, 0))
    per_b = lambda r: pl.BlockSpec((None, r, V_DIM), lambda b, g, pt: (b, 0, 0))
    oatt_s = pl.pallas_call(
        functools.partial(_decode_kernel, n_pg=n_pg, s_dec=S_dec),
        grid_spec=pltpu.PrefetchScalarGridSpec(
            num_scalar_prefetch=1,
            grid=(Bd, n_pages // n_pg),
            in_specs=[per_b(2 * hs)] + [page_spec(j) for j in range(n_pg)] * 2 + [per_b(LANES), per_b(LANES)]
            + [vec_spec2(HEAD_DIM)] * 4 + [vec_spec2(V_DIM)],
            out_specs=per_b(hs),
            scratch_shapes=[pltpu.VMEM((2 * hs, 1), F32), pltpu.VMEM((2 * hs, 1), F32), pltpu.VMEM((2 * hs, V_DIM), F32)],
        ),
        out_shape=jax.ShapeDtypeStruct((Bd, hs, V_DIM), BF16),
        compiler_params=_params("parallel", "arbitrary"),
        name="attn_decode",
    )(page_table, qmat, *([ck] * n_pg), *([cv] * n_pg), kn_pad, vn_pad, lq1, lk1, lq2, lk2, sg)
    oatt_s = jnp.transpose(oatt_s.reshape(Bd, N_HEADS, S_dec, V_DIM), (2, 0, 1, 3)).reshape(n_s, D_ATTN)

    y_p = _moe_call(x_prompt.reshape(B * SEQ, D), oatt_p.reshape(B * SEQ, D_ATTN), opool_p.reshape(B * SEQ, D_POOL),
                    wo, nfg, rw, rb, wg, wu, wd, min(MOE_ROWS, B * SEQ))
    y_s = _moe_call(xs_t, oatt_s, opool_s, wo, nfg, rw, rb, wg, wu, wd, min(MOE_ROWS, n_s))

    y_prompt = y_p.reshape(B, SEQ, D)
    y_sample = jnp.transpose(y_s.reshape(S_dec, Bd, D), (1, 0, 2))
    bc = lambda a: jnp.broadcast_to(a[None], (B,) + a.shape)
    new_k_prompt = jnp.concatenate([bc(k_meta), kf_p], axis=1).reshape(1, B, N_META + SEQ, N_HEADS, V_DIM)
    new_v_prompt = jnp.concatenate([bc(v_meta), vf_p], axis=1).reshape(1, B, N_META + SEQ, N_HEADS, V_DIM)
    new_pool_prompt = utail_p[:, MAX_WINDOW - POOL_STATE:][None]
    to_b = lambda a, w: jnp.transpose(a[:n_s].reshape(S_dec, Bd, w), (1, 0, 2))
    new_k_sample = to_b(k_s, D_QK).reshape(1, Bd, S_dec, N_HEADS, V_DIM)
    new_v_sample = to_b(v_s, D_ATTN).reshape(1, Bd, S_dec, N_HEADS, V_DIM)
    u_new = to_b(u_s, D_POOL)
    new_pool_sample = jnp.concatenate([state_pool[0].astype(F32), u_new], axis=1)[:, S_dec:][None]
    return (y_prompt, y_sample, new_k_prompt, new_v_prompt, new_pool_prompt, new_k_sample, new_v_sample,
            new_pool_sample)
```

```python
import functools
import math

import jax
import jax.numpy as jnp
from jax import lax
from jax.experimental import pallas as pl
from jax.experimental.pallas import tpu as pltpu

N_META = 16
N_HEADS = 4
HEAD_DIM = 64
V_DIM = 2 * HEAD_DIM
D_QK = N_HEADS * 2 * HEAD_DIM
D_ATTN = N_HEADS * V_DIM
POOL_WINDOWS = (2, 4, 8, 16)
POOL_GROUP = 128
MAX_WINDOW = max(POOL_WINDOWS)
POOL_STATE = MAX_WINDOW - 1
D_POOL = POOL_GROUP * len(POOL_WINDOWS)
N_EXPERT_GROUPS = 4
EXPERTS_PER_GROUP = 4
N_EXPERTS = N_EXPERT_GROUPS * EXPERTS_PER_GROUP
ROPE_THETA = 10000.0
EPS = 1e-6
NEG = -1e30
LOG2E = 1.4426950408889634
LAM_INIT = 0.8 - 0.6 * math.exp(-0.3 * 0)
Q_SCALE = HEAD_DIM ** -0.5 * LOG2E

LANES = 128
ROUTER_LANES = LANES
VMEM_LIMIT = 48 * 1024 * 1024

ATTN_ROWS = 512
ATTN_LANE_BLOCK = 256
MOE_ROWS = 512
DECODE_PAGES = 8

F32 = jnp.float32
BF16 = jnp.bfloat16


def _dot(a, b):
    return jnp.dot(a, b, preferred_element_type=F32)


def _dot_nt(a, b):
    return lax.dot_general(a, b, (((1,), (1,)), ((), ())), preferred_element_type=F32)


def _rms(x, g):
    return x * lax.rsqrt(jnp.mean(x * x, axis=-1, keepdims=True) + EPS) * g


def _qk_norm_rope(z, gmat, g_t, cos_t, sin_t):
    ms = _dot((z * z).astype(BF16), gmat)
    xn = z * lax.rsqrt(ms + EPS) * g_t
    outs = []
    for j in range(z.shape[1] // LANES):
        xb = xn[:, j * LANES:(j + 1) * LANES]
        lane = lax.broadcasted_iota(jnp.int32, xb.shape, 1)
        first_half = (lane % HEAD_DIM) < (HEAD_DIM // 2)
        rot = jnp.where(first_half, pltpu.roll(xb, LANES - HEAD_DIM // 2, 1), pltpu.roll(xb, HEAD_DIM // 2, 1))
        outs.append(xb * cos_t + rot * sin_t)
    return jnp.concatenate(outs, axis=1)


def _project(x, ng, win_ref, gmat, qg, kg, cos_t, sin_t):
    h = _rms(x, ng).astype(BF16)
    q = _qk_norm_rope(_dot(h, win_ref[:, 0:D_QK]), gmat, qg, cos_t, sin_t) * Q_SCALE
    k = _qk_norm_rope(_dot(h, win_ref[:, D_QK:2 * D_QK]), gmat, kg, cos_t, sin_t)
    v = _dot(h, win_ref[:, 2 * D_QK:2 * D_QK + D_ATTN])
    u = _dot(h, win_ref[:, 2 * D_QK + D_ATTN:])
    return q, k, v, u


def _proj_small_kernel(x_ref, cos_ref, sin_ref, st_ref, ng_ref, win_ref, gmat_ref, qg_ref, kg_ref, pw_ref, ps_ref,
                       q_ref, k_ref, v_ref, u_ref, op_ref, *, n_dec, s_dec):
    q, k, v, u = _project(x_ref[...], ng_ref[...], win_ref, gmat_ref[...], qg_ref[...], kg_ref[...],
                          cos_ref[...], sin_ref[...])
    q_ref[...] = q
    k_ref[...] = k
    v_ref[...] = v
    u_ref[...] = u
    for g, w in enumerate(POOL_WINDOWS):
        cs = slice(g * POOL_GROUP, (g + 1) * POOL_GROUP)
        us = [u[s * n_dec:(s + 1) * n_dec, cs] for s in range(s_dec)]
        need = {w - 1 - s for s in range(s_dec) if w - 1 - s > 0}
        suffix = {0: jnp.zeros_like(us[0])}
        run = jnp.zeros_like(us[0])
        for c in range(1, max(need) + 1 if need else 1):
            run = run + st_ref[POOL_STATE - c, :, cs]
            if c in need:
                suffix[c] = run
        for s in range(s_dec):
            win = suffix[max(w - 1 - s, 0)]
            for sp in range(max(0, s - w + 1), s + 1):
                win = win + us[sp]
            d = win * (1.0 / w) - us[s]
            o = _dot(d.astype(BF16), pw_ref[g]) * ps_ref[:, cs]
            op_ref[s * n_dec:(s + 1) * n_dec, cs] = o.astype(BF16)


def _proj_prompt_kernel(x_ref, cos_ref, sin_ref, um_ref, ng_ref, win_ref, gmat_ref, qg_ref, kg_ref, pw_ref, ps_ref,
                        q_ref, kb_ref, vt_ref, kf_ref, vf_ref, op_ref, ut_ref, ubuf):
    i = pl.program_id(1)
    t = x_ref.shape[0]
    q, k, v, u = _project(x_ref[...], ng_ref[...], win_ref, gmat_ref[...], qg_ref[...], kg_ref[...],
                          cos_ref[...], sin_ref[...])
    q_ref[...] = q.astype(BF16)
    kf_ref[...] = k
    kb_ref[...] = k.astype(BF16)
    vf_ref[...] = v
    for h in range(N_HEADS):
        vt_ref[h] = v[:, h * V_DIM:(h + 1) * V_DIM].T.astype(BF16)

    @pl.when(i == 0)
    def _():
        ubuf[0:MAX_WINDOW, :] = um_ref[...]

    ubuf[MAX_WINDOW:MAX_WINDOW + t, :] = u
    for g, w in enumerate(POOL_WINDOWS):
        cs = slice(g * POOL_GROUP, (g + 1) * POOL_GROUP)
        e = ubuf[:, cs]
        a = e + pltpu.roll(e, 1, 0)
        sh = 2
        while sh < w:
            a = a + pltpu.roll(a, sh, 0)
            sh *= 2
        d = a[MAX_WINDOW:, :] * (1.0 / w) - e[MAX_WINDOW:, :]
        o = _dot(d.astype(BF16), pw_ref[g]) * ps_ref[:, cs]
        op_ref[:, cs] = o.astype(BF16)
    tail = ubuf[t:t + MAX_WINDOW, :]
    ut_ref[...] = tail
    ubuf[0:MAX_WINDOW, :] = tail


def _diff_lambda(lq1_ref, lk1_ref, lq2_ref, lk2_ref):
    a = jnp.sum(lq1_ref[...] * lk1_ref[...], axis=-1, keepdims=True)
    b = jnp.sum(lq2_ref[...] * lk2_ref[...], axis=-1, keepdims=True)
    return jnp.exp(a) - jnp.exp(b) + LAM_INIT


def _online_update(s, v, m_s, l_s, acc_s):
    m_old = m_s[...]
    m_new = jnp.maximum(m_old, jnp.max(s, axis=-1, keepdims=True))
    alpha = jnp.exp2(m_old - m_new)
    p = jnp.exp2(s - m_new)
    l_s[...] = alpha * l_s[...] + jnp.sum(p, axis=-1, keepdims=True)
    acc_s[...] = alpha * acc_s[...] + _dot(p.astype(BF16), v)
    m_s[...] = m_new


def _diff_finalize(n, lam, sg, l_s, acc_s):
    o = acc_s[0:n, :] / l_s[0:n, :] - lam * (acc_s[n:2 * n, :] / l_s[n:2 * n, :])
    return _rms(o, sg) * (1.0 - LAM_INIT)


def _attn_prompt_kernel(q_ref, k_ref, vt_ref, km_ref, vmt_ref, lq1_ref, lk1_ref, lq2_ref, lk2_ref, sgc_ref,
                        o_ref, qs, m_s, l_s, acc_s):
    qi = pl.program_id(2)
    tq = q_ref.shape[0]
    q = q_ref[...]
    lane = lax.broadcasted_iota(jnp.int32, q.shape, 1)
    zero = jnp.zeros_like(q)
    qs[0:tq, :] = jnp.where(lane < HEAD_DIM, q, zero)
    qs[tq:2 * tq, :] = jnp.where(lane >= HEAD_DIM, q, zero)
    n_blk = 2 * tq // ATTN_LANE_BLOCK

    blocks = [slice(j * ATTN_LANE_BLOCK, (j + 1) * ATTN_LANE_BLOCK) for j in range(n_blk)]
    m_s[...] = jnp.full(m_s.shape, NEG, F32)
    l_s[...] = jnp.zeros(l_s.shape, F32)
    acc_s[...] = jnp.zeros(acc_s.shape, F32)

    def scores(kk):
        k = k_ref[pl.ds(pl.multiple_of(kk * tq, tq), tq), :]
        return tuple(_dot_nt(k, qs[cs, :]) for cs in blocks)

    def update(cs, parts):
        m_old = m_s[:, cs]
        m_new = m_old
        for s, _ in parts:
            m_new = jnp.maximum(m_new, jnp.max(s, axis=0, keepdims=True))
        alpha = jnp.exp2(m_old - m_new)
        l = alpha * l_s[:, cs]
        acc = alpha * acc_s[:, cs]
        for s, vt in parts:
            p = jnp.exp2(s - m_new)
            l = l + jnp.sum(p, axis=0, keepdims=True)
            acc = acc + _dot(vt, p.astype(BF16))
        l_s[:, cs] = l
        acc_s[:, cs] = acc
        m_s[:, cs] = m_new

    def body(kk, carry):
        s_all = scores(kk)
        vt = vt_ref[kk]
        for cs, s in zip(blocks, s_all):
            update(cs, [(s, vt)])
        return carry

    lax.fori_loop(0, qi, body, 0)
    s_diag = scores(qi)
    vt = vt_ref[qi]
    for j, cs in enumerate(blocks):
        s_meta = _dot_nt(km_ref[...], qs[cs, :])
        mkey = lax.broadcasted_iota(jnp.int32, s_meta.shape, 0)
        s_meta = jnp.where(mkey < N_META, s_meta, NEG)
        s = s_diag[j]
        key = lax.broadcasted_iota(jnp.int32, s.shape, 0)
        ql = lax.broadcasted_iota(jnp.int32, s.shape, 1) + j * ATTN_LANE_BLOCK
        tok = jnp.where(ql >= tq, ql - tq, ql)
        s = jnp.where(key <= tok, s, NEG)
        update(cs, [(s_meta, vmt_ref[...]), (s, vt)])
    lam = _diff_lambda(lq1_ref, lk1_ref, lq2_ref, lk2_ref)
    o = acc_s[:, 0:tq] / l_s[:, 0:tq] - lam * (acc_s[:, tq:2 * tq] / l_s[:, tq:2 * tq])
    on = o * lax.rsqrt(jnp.mean(o * o, axis=0, keepdims=True) + EPS) * sgc_ref[...] * (1.0 - LAM_INIT)
    o_ref[...] = on.T.astype(BF16)


def _decode_kernel(pt_ref, q_ref, *rest, n_pg, s_dec):
    k_refs = rest[:n_pg]
    v_refs = rest[n_pg:2 * n_pg]
    (kn_ref, vn_ref, lq1_ref, lk1_ref, lq2_ref, lk2_ref, sg_ref, o_ref, m_s, l_s, acc_s) = rest[2 * n_pg:]
    del pt_ref
    g = pl.program_id(1)
    n_rows = q_ref.shape[0]
    half = n_rows // 2

    @pl.when(g == 0)
    def _():
        m_s[...] = jnp.full(m_s.shape, NEG, F32)
        l_s[...] = jnp.zeros(l_s.shape, F32)
        acc_s[...] = jnp.zeros(acc_s.shape, F32)

    q = q_ref[...]
    page_cols = k_refs[0].shape[0]
    row = lax.broadcasted_iota(jnp.int32, (n_rows, page_cols), 0)
    col = lax.broadcasted_iota(jnp.int32, (n_rows, page_cols), 1)
    head_ok = (col % N_HEADS) == ((row % half) // s_dec)
    ss = []
    for j in range(n_pg):
        s = _dot_nt(q, k_refs[j][...].astype(BF16))
        ss.append(jnp.where(head_ok, s, NEG))
    m_old = m_s[...]
    m_new = m_old
    for s in ss:
        m_new = jnp.maximum(m_new, jnp.max(s, axis=-1, keepdims=True))
    alpha = jnp.exp2(m_old - m_new)
    l = alpha * l_s[...]
    acc = alpha * acc_s[...]
    for j in range(n_pg):
        p = jnp.exp2(ss[j] - m_new)
        l = l + jnp.sum(p, axis=-1, keepdims=True)
        acc = acc + _dot(p.astype(BF16), v_refs[j][...].astype(BF16))
    m_s[...] = m_new
    l_s[...] = l
    acc_s[...] = acc

    @pl.when(g == pl.num_programs(1) - 1)
    def _():
        s = _dot_nt(q, kn_ref[...])
        r2 = lax.broadcasted_iota(jnp.int32, s.shape, 0)
        c2 = lax.broadcasted_iota(jnp.int32, s.shape, 1)
        ok = (c2 < s_dec * N_HEADS) & ((c2 % N_HEADS) == ((r2 % half) // s_dec)) & ((c2 // N_HEADS) <= (r2 % s_dec))
        _online_update(jnp.where(ok, s, NEG), vn_ref[...], m_s, l_s, acc_s)
        lam = _diff_lambda(lq1_ref, lk1_ref, lq2_ref, lk2_ref)
        o_ref[...] = _diff_finalize(half, lam, sg_ref[...], l_s, acc_s).astype(BF16)


def _route(logits):
    ninf = float("-inf")
    lane = lax.broadcasted_iota(jnp.int32, logits.shape, 1)
    lane_f = lane.astype(F32)
    big = float(ROUTER_LANES)
    is_g = lane < N_EXPERT_GROUPS
    gl = jnp.where(is_g, logits, ninf)
    gmax = jnp.max(gl, axis=-1, keepdims=True)
    gidx = jnp.min(jnp.where(gl == gmax, lane_f, big), axis=-1, keepdims=True)
    gsum = jnp.sum(jnp.where(is_g, jnp.exp(logits - gmax), 0.0), axis=-1, keepdims=True)
    gprob = 1.0 / gsum
    egrp = ((lane - N_EXPERT_GROUPS) // EXPERTS_PER_GROUP).astype(F32)
    is_e = (lane >= N_EXPERT_GROUPS) & (lane < N_EXPERT_GROUPS + N_EXPERTS) & (egrp == gidx)
    ev = jnp.where(is_e, logits, ninf)
    t1 = jnp.max(ev, axis=-1, keepdims=True)
    i1 = jnp.min(jnp.where(ev == t1, lane_f, big), axis=-1, keepdims=True)
    ev2 = jnp.where(lane_f == i1, ninf, ev)
    t2 = jnp.max(ev2, axis=-1, keepdims=True)
    i2 = jnp.min(jnp.where(ev2 == t2, lane_f, big), axis=-1, keepdims=True)
    r = jnp.exp(t2 - t1)
    w1 = gprob / (1.0 + r)
    w2 = w1 * r
    return jnp.where(lane_f == i1, w1, 0.0) + jnp.where(lane_f == i2, w2, 0.0)


def _moe_kernel(x_ref, oa_ref, op_ref, wo_ref, ng_ref, rw_ref, rb_ref, wg_ref, wu_ref, wd_ref,
                y_ref, h2_s, comb_s, acc_s):
    e = pl.program_id(1)

    @pl.when(e == 0)
    def _():
        x1 = x_ref[...] + _dot(oa_ref[...], wo_ref[0:D_ATTN, :]) + _dot(op_ref[...], wo_ref[D_ATTN:, :])
        h2 = _rms(x1, ng_ref[...])
        h2_s[...] = h2.astype(BF16)
        logits = jnp.dot(h2, rw_ref[...], preferred_element_type=F32, precision=lax.Precision.HIGHEST) + rb_ref[...]
        comb_s[...] = _route(logits)
        acc_s[...] = x1

    hb = h2_s[...]
    gate = _dot(hb, wg_ref[...])
    a = gate * (1.0 / (1.0 + jnp.exp(-gate))) * _dot(hb, wu_ref[...])
    lane = lax.broadcasted_iota(jnp.int32, comb_s.shape, 1)
    c = jnp.sum(jnp.where(lane == e + N_EXPERT_GROUPS, comb_s[...], 0.0), axis=-1, keepdims=True)
    acc_s[...] += _dot((a * c).astype(BF16), wd_ref[...])

    @pl.when(e == pl.num_programs(1) - 1)
    def _():
        y_ref[...] = acc_s[...]


def _params(*sem):
    return pltpu.CompilerParams(dimension_semantics=sem, vmem_limit_bytes=VMEM_LIMIT)


def _full(shape):
    return pl.BlockSpec(shape, lambda *_: (0,) * len(shape))


def _rope_tables(pos):
    inv_freq = ROPE_THETA ** (-jnp.arange(0, HEAD_DIM, 2, dtype=F32) / HEAD_DIM)
    ang = pos.astype(F32)[:, None] * inv_freq[None, :]
    cos, sin = jnp.cos(ang), jnp.sin(ang)
    reps = LANES // HEAD_DIM
    return jnp.tile(jnp.concatenate([cos, cos], 1), (1, reps)), jnp.tile(jnp.concatenate([-sin, sin], 1), (1, reps))


def _moe_call(x, oa, op, wo, ng, rw, rb, wg, wu, wd, rows):
    n, d = x.shape
    f = wg.shape[-1]
    return pl.pallas_call(
        _moe_kernel,
        grid=(n // rows, N_EXPERTS),
        in_specs=[
            pl.BlockSpec((rows, d), lambda i, e: (i, 0)),
            pl.BlockSpec((rows, D_ATTN), lambda i, e: (i, 0)),
            pl.BlockSpec((rows, D_POOL), lambda i, e: (i, 0)),
            _full(wo.shape), _full(ng.shape), _full(rw.shape), _full(rb.shape),
            pl.BlockSpec((None, d, f), lambda i, e: (e, 0, 0)),
            pl.BlockSpec((None, d, f), lambda i, e: (e, 0, 0)),
            pl.BlockSpec((None, f, d), lambda i, e: (e, 0, 0)),
        ],
        out_specs=pl.BlockSpec((rows, d), lambda i, e: (i, 0)),
        out_shape=jax.ShapeDtypeStruct((n, d), F32),
        scratch_shapes=[pltpu.VMEM((rows, d), BF16), pltpu.VMEM((rows, ROUTER_LANES), F32), pltpu.VMEM((rows, d), F32)],
        compiler_params=_params("parallel", "arbitrary"),
        name="merge_moe",
    )(x, oa, op, wo, ng, rw, rb, wg, wu, wd)


def kernel(x_prompt, x_sample, cache_k, cache_v, state_pool, page_table, meta_tokens, norm_attn_g, w_in, q_norm_g, k_norm_g, lambda_q1, lambda_k1, lambda_q2, lambda_k2, subln_g, pool_w, pool_scale, w_out, norm_ffn_g, router_group_w, router_group_b, router_expert_w, router_expert_b, expert_w_gate, expert_w_up, expert_w_down):
    B, SEQ, D = x_prompt.shape
    Bd, S_dec, _ = x_sample.shape
    n_phys, page = cache_k.shape[1], cache_k.shape[2]
    n_pages = page_table.shape[1]
    past_len = n_pages * page
    assert cache_k.shape[0] == 1 and N_META == MAX_WINDOW

    ng = norm_attn_g[0][None, :]
    win = w_in[0].astype(BF16)
    reps = D_QK // HEAD_DIM
    qg = jnp.tile(q_norm_g[0], reps)[None, :]
    kg = jnp.tile(k_norm_g[0], reps)[None, :]
    grp = jnp.arange(D_QK) // HEAD_DIM
    gmat = jnp.where(grp[:, None] == grp[None, :], 1.0 / HEAD_DIM, 0.0).astype(BF16)
    pw = pool_w[0].astype(BF16)
    ps = pool_scale[0][None, :]
    lq1, lk1, lq2, lk2 = (a[0][None, :] for a in (lambda_q1, lambda_k1, lambda_q2, lambda_k2))
    sg = subln_g[0][None, :]
    wo = w_out[0].astype(BF16)
    nfg = norm_ffn_g[0][None, :]
    pad = ROUTER_LANES - N_EXPERT_GROUPS - N_EXPERTS
    rw = jnp.concatenate([router_group_w[0], router_expert_w[0], jnp.zeros((D, pad), F32)], axis=1)
    rb = jnp.concatenate([router_group_b[0], router_expert_b[0], jnp.zeros((pad,), F32)])[None, :]
    wg = expert_w_gate[0].astype(BF16)
    wu = expert_w_up[0].astype(BF16)
    wd = expert_w_down[0].astype(BF16)

    n_s = Bd * S_dec
    n_small = n_s + N_META
    xs_t = jnp.transpose(x_sample, (1, 0, 2)).reshape(n_s, D)
    x_small = jnp.concatenate([xs_t, meta_tokens.astype(F32)], axis=0)
    pos_small = jnp.concatenate([jnp.repeat(past_len + jnp.arange(S_dec), Bd), jnp.arange(N_META)])
    cos_s, sin_s = _rope_tables(pos_small)
    st_t = jnp.transpose(state_pool[0], (1, 0, 2))
    q_s, k_s, v_s, u_s, opool_s = pl.pallas_call(
        functools.partial(_proj_small_kernel, n_dec=Bd, s_dec=S_dec),
        out_shape=[jax.ShapeDtypeStruct((n_small, D_QK), F32)] * 4 + [jax.ShapeDtypeStruct((n_s, D_POOL), BF16)],
        compiler_params=pltpu.CompilerParams(vmem_limit_bytes=VMEM_LIMIT),
        name="proj_small",
    )(x_small, cos_s, sin_s, st_t, ng, win, gmat, qg, kg, pw, ps)
    k_meta, v_meta, u_meta = k_s[n_s:], v_s[n_s:], u_s[n_s:]

    tq = min(ATTN_ROWS, SEQ)
    tp = tq
    cos_p, sin_p = _rope_tables(N_META + jnp.arange(SEQ))
    row_spec = lambda w: pl.BlockSpec((None, tp, w), lambda b, i: (b, i, 0))
    tab_spec = pl.BlockSpec((tp, LANES), lambda b, i: (i, 0))
    vt_spec = pl.BlockSpec((None, N_HEADS, None, V_DIM, tp), lambda b, i: (b, 0, i, 0, 0))
    q_p, kb_p, vt_p, kf_p, vf_p, opool_p, utail_p = pl.pallas_call(
        _proj_prompt_kernel,
        grid=(B, SEQ // tp),
        in_specs=[row_spec(D), tab_spec, tab_spec, _full(u_meta.shape), _full(ng.shape), _full(win.shape),
                  _full(gmat.shape), _full(qg.shape), _full(kg.shape), _full(pw.shape), _full(ps.shape)],
        out_specs=[row_spec(D_QK), row_spec(D_QK), vt_spec, row_spec(D_QK), row_spec(D_ATTN), row_spec(D_POOL),
                   pl.BlockSpec((None, MAX_WINDOW, D_POOL), lambda b, i: (b, 0, 0))],
        out_shape=[jax.ShapeDtypeStruct((B, SEQ, D_QK), BF16)] * 2
        + [jax.ShapeDtypeStruct((B, N_HEADS, SEQ // tp, V_DIM, tp), BF16)]
        + [jax.ShapeDtypeStruct((B, SEQ, D_QK), F32)] * 2
        + [jax.ShapeDtypeStruct((B, SEQ, D_POOL), BF16), jax.ShapeDtypeStruct((B, MAX_WINDOW, D_POOL), F32)],
        scratch_shapes=[pltpu.VMEM((MAX_WINDOW + tp, D_POOL), F32)],
        compiler_params=_params("parallel", "arbitrary"),
        name="proj_prompt",
    )(x_prompt, cos_p, sin_p, u_meta, ng, win, gmat, qg, kg, pw, ps)

    km_pad = jnp.zeros((LANES, D_QK), BF16).at[:N_META].set(k_meta.astype(BF16))
    vm_pad = jnp.zeros((LANES, D_ATTN), BF16).at[:N_META].set(v_meta.astype(BF16))
    vmt_pad = jnp.transpose(vm_pad.reshape(LANES, N_HEADS, V_DIM), (1, 2, 0))
    vec_spec3 = lambda w: pl.BlockSpec((1, w), lambda b, h, i: (0, 0))
    oatt_p = pl.pallas_call(
        _attn_prompt_kernel,
        grid=(B, N_HEADS, SEQ // tq),
        in_specs=[
            pl.BlockSpec((None, tq, V_DIM), lambda b, h, i: (b, i, h)),
            pl.BlockSpec((None, SEQ, V_DIM), lambda b, h, i: (b, 0, h)),
            pl.BlockSpec((None, None, SEQ // tq, V_DIM, tq), lambda b, h, i: (b, h, 0, 0, 0)),
            pl.BlockSpec((LANES, V_DIM), lambda b, h, i: (0, h)),
            pl.BlockSpec((None, V_DIM, LANES), lambda b, h, i: (h, 0, 0)),
            vec_spec3(HEAD_DIM), vec_spec3(HEAD_DIM), vec_spec3(HEAD_DIM), vec_spec3(HEAD_DIM),
            pl.BlockSpec((V_DIM, 1), lambda b, h, i: (0, 0)),
        ],
        out_specs=pl.BlockSpec((None, tq, V_DIM), lambda b, h, i: (b, i, h)),
        out_shape=jax.ShapeDtypeStruct((B, SEQ, D_ATTN), BF16),
        scratch_shapes=[pltpu.VMEM((2 * tq, V_DIM), BF16), pltpu.VMEM((1, 2 * tq), F32), pltpu.VMEM((1, 2 * tq), F32),
                        pltpu.VMEM((V_DIM, 2 * tq), F32)],
        compiler_params=_params("parallel", "parallel", "arbitrary"),
        name="attn_prompt",
    )(q_p, kb_p, vt_p, km_pad, vmt_pad, lq1, lk1, lq2, lk2, sg.reshape(V_DIM, 1))

    hs = N_HEADS * S_dec
    qd = q_s[:n_s].reshape(S_dec, Bd, N_HEADS, 2, HEAD_DIM)
    qd = jnp.transpose(qd, (1, 3, 2, 0, 4))
    qmat = jnp.zeros((Bd, 2, hs, 2, HEAD_DIM), F32)
    qmat = qmat.at[:, 0, :, 0].set(qd[:, 0].reshape(Bd, hs, HEAD_DIM)).at[:, 1, :, 1].set(qd[:, 1].reshape(Bd, hs, HEAD_DIM))
    qmat = qmat.reshape(Bd, 2 * hs, V_DIM).astype(BF16)
    new_rows = lambda a: jnp.transpose(a[:n_s].reshape(S_dec, Bd, hs // S_dec * V_DIM), (1, 0, 2)).reshape(Bd, hs, V_DIM)
    kn_pad = jnp.zeros((Bd, LANES, V_DIM), BF16).at[:, :hs].set(new_rows(k_s).astype(BF16))
    vn_pad = jnp.zeros((Bd, LANES, V_DIM), BF16).at[:, :hs].set(new_rows(v_s).astype(BF16))
    ck = cache_k.reshape(n_phys, page * N_HEADS, V_DIM)
    cv = cache_v.reshape(n_phys, page * N_HEADS, V_DIM)
    n_pg = math.gcd(DECODE_PAGES, n_pages)
    page_spec = lambda j: pl.BlockSpec((None, page * N_HEADS, V_DIM), lambda b, g, pt: (pt[b, g * n_pg + j], 0, 0))
    vec_spec2 = lambda w: pl.BlockSpec((1, w), lambda b, g, pt: (0, 0))
    per_b = lambda r: pl.BlockSpec((None, r, V_DIM), lambda b, g, pt: (b, 0, 0))
    oatt_s = pl.pallas_call(
        functools.partial(_decode_kernel, n_pg=n_pg, s_dec=S_dec),
        grid_spec=pltpu.PrefetchScalarGridSpec(
            num_scalar_prefetch=1,
            grid=(Bd, n_pages // n_pg),
            in_specs=[per_b(2 * hs)] + [page_spec(j) for j in range(n_pg)] * 2 + [per_b(LANES), per_b(LANES)]
            + [vec_spec2(HEAD_DIM)] * 4 + [vec_spec2(V_DIM)],
            out_specs=per_b(hs),
            scratch_shapes=[pltpu.VMEM((2 * hs, 1), F32), pltpu.VMEM((2 * hs, 1), F32), pltpu.VMEM((2 * hs, V_DIM), F32)],
        ),
        out_shape=jax.ShapeDtypeStruct((Bd, hs, V_DIM), BF16),
        compiler_params=_params("parallel", "arbitrary"),
        name="attn_decode",
    )(page_table, qmat, *([ck] * n_pg), *([cv] * n_pg), kn_pad, vn_pad, lq1, lk1, lq2, lk2, sg)
    oatt_s = jnp.transpose(oatt_s.reshape(Bd, N_HEADS, S_dec, V_DIM), (2, 0, 1, 3)).reshape(n_s, D_ATTN)

    y_p = _moe_call(x_prompt.reshape(B * SEQ, D), oatt_p.reshape(B * SEQ, D_ATTN), opool_p.reshape(B * SEQ, D_POOL),
                    wo, nfg, rw, rb, wg, wu, wd, min(MOE_ROWS, B * SEQ))
    y_s = _moe_call(xs_t, oatt_s, opool_s, wo, nfg, rw, rb, wg, wu, wd, min(MOE_ROWS, n_s))

    y_prompt = y_p.reshape(B, SEQ, D)
    y_sample = jnp.transpose(y_s.reshape(S_dec, Bd, D), (1, 0, 2))
    bc = lambda a: jnp.broadcast_to(a[None], (B,) + a.shape)
    new_k_prompt = jnp.concatenate([bc(k_meta), kf_p], axis=1).reshape(1, B, N_META + SEQ, N_HEADS, V_DIM)
    new_v_prompt = jnp.concatenate([bc(v_meta), vf_p], axis=1).reshape(1, B, N_META + SEQ, N_HEADS, V_DIM)
    new_pool_prompt = utail_p[:, MAX_WINDOW - POOL_STATE:][None]
    to_b = lambda a, w: jnp.transpose(a[:n_s].reshape(S_dec, Bd, w), (1, 0, 2))
    new_k_sample = to_b(k_s, D_QK).reshape(1, Bd, S_dec, N_HEADS, V_DIM)
    new_v_sample = to_b(v_s, D_ATTN).reshape(1, Bd, S_dec, N_HEADS, V_DIM)
    u_new = to_b(u_s, D_POOL)
    new_pool_sample = jnp.concatenate([state_pool[0].astype(F32), u_new], axis=1)[:, S_dec:][None]
    return (y_prompt, y_sample, new_k_prompt, new_v_prompt, new_pool_prompt, new_k_sample, new_v_sample,
            new_pool_sample)
```

```python
import functools
import math

import jax
import jax.numpy as jnp
from jax import lax
from jax.experimental import pallas as pl
from jax.experimental.pallas import tpu as pltpu

N_META = 16
N_HEADS = 4
HEAD_DIM = 64
V_DIM = 2 * HEAD_DIM
D_QK = N_HEADS * 2 * HEAD_DIM
D_ATTN = N_HEADS * V_DIM
POOL_WINDOWS = (2, 4, 8, 16)
POOL_GROUP = 128
MAX_WINDOW = max(POOL_WINDOWS)
POOL_STATE = MAX_WINDOW - 1
D_POOL = POOL_GROUP * len(POOL_WINDOWS)
N_EXPERT_GROUPS = 4
EXPERTS_PER_GROUP = 4
N_EXPERTS = N_EXPERT_GROUPS * EXPERTS_PER_GROUP
ROPE_THETA = 10000.0
EPS = 1e-6
NEG = -1e30
LOG2E = 1.4426950408889634
LAM_INIT = 0.8 - 0.6 * math.exp(-0.3 * 0)
Q_SCALE = HEAD_DIM ** -0.5 * LOG2E

LANES = 128
ROUTER_LANES = LANES
VMEM_LIMIT = 48 * 1024 * 1024

ATTN_ROWS = 512
ATTN_LANE_BLOCK = 256
MOE_ROWS = 512
MOE_CHUNK = 512
ROW_DMA_UNROLL = 8
DECODE_PAGES = 8

F32 = jnp.float32
BF16 = jnp.bfloat16


def _dot(a, b):
    return jnp.dot(a, b, preferred_element_type=F32)


def _dot_nt(a, b):
    return lax.dot_general(a, b, (((1,), (1,)), ((), ())), preferred_element_type=F32)


def _rms(x, g):
    return x * lax.rsqrt(jnp.mean(x * x, axis=-1, keepdims=True) + EPS) * g


def _qk_norm_rope(z, gmat, g_t, cos_t, sin_t):
    ms = _dot((z * z).astype(BF16), gmat)
    xn = z * lax.rsqrt(ms + EPS) * g_t
    outs = []
    for j in range(z.shape[1] // LANES):
        xb = xn[:, j * LANES:(j + 1) * LANES]
        lane = lax.broadcasted_iota(jnp.int32, xb.shape, 1)
        first_half = (lane % HEAD_DIM) < (HEAD_DIM // 2)
        rot = jnp.where(first_half, pltpu.roll(xb, LANES - HEAD_DIM // 2, 1), pltpu.roll(xb, HEAD_DIM // 2, 1))
        outs.append(xb * cos_t + rot * sin_t)
    return jnp.concatenate(outs, axis=1)


def _project(x, ng, win_ref, gmat, qg, kg, cos_t, sin_t):
    h = _rms(x, ng).astype(BF16)
    q = _qk_norm_rope(_dot(h, win_ref[:, 0:D_QK]), gmat, qg, cos_t, sin_t) * Q_SCALE
    k = _qk_norm_rope(_dot(h, win_ref[:, D_QK:2 * D_QK]), gmat, kg, cos_t, sin_t)
    v = _dot(h, win_ref[:, 2 * D_QK:2 * D_QK + D_ATTN])
    u = _dot(h, win_ref[:, 2 * D_QK + D_ATTN:])
    return q, k, v, u


def _proj_small_kernel(x_ref, cos_ref, sin_ref, st_ref, ng_ref, win_ref, gmat_ref, qg_ref, kg_ref, pw_ref, ps_ref,
                       q_ref, k_ref, v_ref, u_ref, op_ref, *, n_dec, s_dec):
    q, k, v, u = _project(x_ref[...], ng_ref[...], win_ref, gmat_ref[...], qg_ref[...], kg_ref[...],
                          cos_ref[...], sin_ref[...])
    q_ref[...] = q
    k_ref[...] = k
    v_ref[...] = v
    u_ref[...] = u
    for g, w in enumerate(POOL_WINDOWS):
        cs = slice(g * POOL_GROUP, (g + 1) * POOL_GROUP)
        us = [u[s * n_dec:(s + 1) * n_dec, cs] for s in range(s_dec)]
        need = {w - 1 - s for s in range(s_dec) if w - 1 - s > 0}
        suffix = {0: jnp.zeros_like(us[0])}
        run = jnp.zeros_like(us[0])
        for c in range(1, max(need) + 1 if need else 1):
            run = run + st_ref[POOL_STATE - c, :, cs]
            if c in need:
                suffix[c] = run
        for s in range(s_dec):
            win = suffix[max(w - 1 - s, 0)]
            for sp in range(max(0, s - w + 1), s + 1):
                win = win + us[sp]
            d = win * (1.0 / w) - us[s]
            o = _dot(d.astype(BF16), pw_ref[g]) * ps_ref[:, cs]
            op_ref[s * n_dec:(s + 1) * n_dec, cs] = o.astype(BF16)


def _proj_prompt_kernel(x_ref, cos_ref, sin_ref, um_ref, ng_ref, win_ref, gmat_ref, qg_ref, kg_ref, pw_ref, ps_ref,
                        q_ref, kb_ref, vt_ref, kf_ref, vf_ref, op_ref, ut_ref, ubuf):
    i = pl.program_id(1)
    t = x_ref.shape[0]
    q, k, v, u = _project(x_ref[...], ng_ref[...], win_ref, gmat_ref[...], qg_ref[...], kg_ref[...],
                          cos_ref[...], sin_ref[...])
    q_ref[...] = q.astype(BF16)
    kf_ref[...] = k
    kb_ref[...] = k.astype(BF16)
    vf_ref[...] = v
    for h in range(N_HEADS):
        vt_ref[h] = v[:, h * V_DIM:(h + 1) * V_DIM].T.astype(BF16)

    @pl.when(i == 0)
    def _():
        ubuf[0:MAX_WINDOW, :] = um_ref[...]

    ubuf[MAX_WINDOW:MAX_WINDOW + t, :] = u
    for g, w in enumerate(POOL_WINDOWS):
        cs = slice(g * POOL_GROUP, (g + 1) * POOL_GROUP)
        e = ubuf[:, cs]
        a = e + pltpu.roll(e, 1, 0)
        sh = 2
        while sh < w:
            a = a + pltpu.roll(a, sh, 0)
            sh *= 2
        d = a[MAX_WINDOW:, :] * (1.0 / w) - e[MAX_WINDOW:, :]
        o = _dot(d.astype(BF16), pw_ref[g]) * ps_ref[:, cs]
        op_ref[:, cs] = o.astype(BF16)
    tail = ubuf[t:t + MAX_WINDOW, :]
    ut_ref[...] = tail
    ubuf[0:MAX_WINDOW, :] = tail


def _diff_lambda(lq1_ref, lk1_ref, lq2_ref, lk2_ref):
    a = jnp.sum(lq1_ref[...] * lk1_ref[...], axis=-1, keepdims=True)
    b = jnp.sum(lq2_ref[...] * lk2_ref[...], axis=-1, keepdims=True)
    return jnp.exp(a) - jnp.exp(b) + LAM_INIT


def _online_update(s, v, m_s, l_s, acc_s):
    m_old = m_s[...]
    m_new = jnp.maximum(m_old, jnp.max(s, axis=-1, keepdims=True))
    alpha = jnp.exp2(m_old - m_new)
    p = jnp.exp2(s - m_new)
    l_s[...] = alpha * l_s[...] + jnp.sum(p, axis=-1, keepdims=True)
    acc_s[...] = alpha * acc_s[...] + _dot(p.astype(BF16), v)
    m_s[...] = m_new


def _diff_finalize(n, lam, sg, l_s, acc_s):
    o = acc_s[0:n, :] / l_s[0:n, :] - lam * (acc_s[n:2 * n, :] / l_s[n:2 * n, :])
    return _rms(o, sg) * (1.0 - LAM_INIT)


def _attn_prompt_kernel(q_ref, k_ref, vt_ref, km_ref, vmt_ref, lq1_ref, lk1_ref, lq2_ref, lk2_ref, sgc_ref,
                        o_ref, qs, m_s, l_s, acc_s, sa, sb):
    qi = pl.program_id(2)
    tq = q_ref.shape[0]
    q = q_ref[...]
    lane = lax.broadcasted_iota(jnp.int32, q.shape, 1)
    zero = jnp.zeros_like(q)
    qs[0:tq, :] = jnp.where(lane < HEAD_DIM, q, zero)
    qs[tq:2 * tq, :] = jnp.where(lane >= HEAD_DIM, q, zero)
    n_blk = 2 * tq // ATTN_LANE_BLOCK

    blocks = [slice(j * ATTN_LANE_BLOCK, (j + 1) * ATTN_LANE_BLOCK) for j in range(n_blk)]
    m_s[...] = jnp.full(m_s.shape, NEG, F32)
    l_s[...] = jnp.zeros(l_s.shape, F32)
    acc_s[...] = jnp.zeros(acc_s.shape, F32)

    def scores_to(kk, buf):
        k = k_ref[pl.ds(pl.multiple_of(kk * tq, tq), tq), :]
        for cs in blocks:
            buf[:, cs] = _dot_nt(k, qs[cs, :])

    def update(cs, parts):
        m_old = m_s[:, cs]
        m_new = m_old
        for s, _ in parts:
            m_new = jnp.maximum(m_new, jnp.max(s, axis=0, keepdims=True))
        alpha = jnp.exp2(m_old - m_new)
        l = alpha * l_s[:, cs]
        acc = alpha * acc_s[:, cs]
        for s, vt in parts:
            p = jnp.exp2(s - m_new)
            l = l + jnp.sum(p, axis=0, keepdims=True)
            acc = acc + _dot(vt, p.astype(BF16))
        l_s[:, cs] = l
        acc_s[:, cs] = acc
        m_s[:, cs] = m_new

    def process(buf, kk):
        vt = vt_ref[kk]
        for cs in blocks:
            update(cs, [(buf[:, cs], vt)])

    def last_step(buf):
        vt = vt_ref[qi]
        for j, cs in enumerate(blocks):
            s_meta = _dot_nt(km_ref[...], qs[cs, :])
            mkey = lax.broadcasted_iota(jnp.int32, s_meta.shape, 0)
            s_meta = jnp.where(mkey < N_META, s_meta, NEG)
            s = buf[:, cs]
            key = lax.broadcasted_iota(jnp.int32, s.shape, 0)
            ql = lax.broadcasted_iota(jnp.int32, s.shape, 1) + j * ATTN_LANE_BLOCK
            tok = jnp.where(ql >= tq, ql - tq, ql)
            s = jnp.where(key <= tok, s, NEG)
            update(cs, [(s_meta, vmt_ref[...]), (s, vt)])

    scores_to(0, sa)

    def pair(pp, carry):
        k0 = 2 * pp
        scores_to(k0 + 1, sb)
        process(sa, k0)
        scores_to(k0 + 2, sa)
        process(sb, k0 + 1)
        return carry

    lax.fori_loop(0, qi // 2, pair, 0)

    @pl.when(qi % 2 == 1)
    def _():
        scores_to(qi, sb)
        process(sa, qi - 1)
        last_step(sb)

    @pl.when(qi % 2 == 0)
    def _():
        last_step(sa)

    lam = _diff_lambda(lq1_ref, lk1_ref, lq2_ref, lk2_ref)
    o = acc_s[:, 0:tq] / l_s[:, 0:tq] - lam * (acc_s[:, tq:2 * tq] / l_s[:, tq:2 * tq])
    on = o * lax.rsqrt(jnp.mean(o * o, axis=0, keepdims=True) + EPS) * sgc_ref[...] * (1.0 - LAM_INIT)
    o_ref[...] = on.T.astype(BF16)


def _decode_kernel(pt_ref, q_ref, *rest, n_pg, s_dec):
    k_refs = rest[:n_pg]
    v_refs = rest[n_pg:2 * n_pg]
    (kn_ref, vn_ref, lq1_ref, lk1_ref, lq2_ref, lk2_ref, sg_ref, o_ref, m_s, l_s, acc_s) = rest[2 * n_pg:]
    del pt_ref
    g = pl.program_id(1)
    n_rows = q_ref.shape[0]
    half = n_rows // 2

    @pl.when(g == 0)
    def _():
        m_s[...] = jnp.full(m_s.shape, NEG, F32)
        l_s[...] = jnp.zeros(l_s.shape, F32)
        acc_s[...] = jnp.zeros(acc_s.shape, F32)

    q = q_ref[...]
    page_cols = k_refs[0].shape[0]
    row = lax.broadcasted_iota(jnp.int32, (n_rows, page_cols), 0)
    col = lax.broadcasted_iota(jnp.int32, (n_rows, page_cols), 1)
    head_ok = (col % N_HEADS) == ((row % half) // s_dec)
    ss = []
    for j in range(n_pg):
        s = _dot_nt(q, k_refs[j][...].astype(BF16))
        ss.append(jnp.where(head_ok, s, NEG))
    m_old = m_s[...]
    m_new = m_old
    for s in ss:
        m_new = jnp.maximum(m_new, jnp.max(s, axis=-1, keepdims=True))
    alpha = jnp.exp2(m_old - m_new)
    l = alpha * l_s[...]
    acc = alpha * acc_s[...]
    for j in range(n_pg):
        p = jnp.exp2(ss[j] - m_new)
        l = l + jnp.sum(p, axis=-1, keepdims=True)
        acc = acc + _dot(p.astype(BF16), v_refs[j][...].astype(BF16))
    m_s[...] = m_new
    l_s[...] = l
    acc_s[...] = acc

    @pl.when(g == pl.num_programs(1) - 1)
    def _():
        s = _dot_nt(q, kn_ref[...])
        r2 = lax.broadcasted_iota(jnp.int32, s.shape, 0)
        c2 = lax.broadcasted_iota(jnp.int32, s.shape, 1)
        ok = (c2 < s_dec * N_HEADS) & ((c2 % N_HEADS) == ((r2 % half) // s_dec)) & ((c2 // N_HEADS) <= (r2 % s_dec))
        _online_update(jnp.where(ok, s, NEG), vn_ref[...], m_s, l_s, acc_s)
        lam = _diff_lambda(lq1_ref, lk1_ref, lq2_ref, lk2_ref)
        o_ref[...] = _diff_finalize(half, lam, sg_ref[...], l_s, acc_s).astype(BF16)


def _route(logits):
    ninf = float("-inf")
    lane = lax.broadcasted_iota(jnp.int32, logits.shape, 1)
    lane_f = lane.astype(F32)
    big = float(ROUTER_LANES)
    is_g = lane < N_EXPERT_GROUPS
    gl = jnp.where(is_g, logits, ninf)
    gmax = jnp.max(gl, axis=-1, keepdims=True)
    gidx = jnp.min(jnp.where(gl == gmax, lane_f, big), axis=-1, keepdims=True)
    gsum = jnp.sum(jnp.where(is_g, jnp.exp(logits - gmax), 0.0), axis=-1, keepdims=True)
    gprob = 1.0 / gsum
    egrp = ((lane - N_EXPERT_GROUPS) // EXPERTS_PER_GROUP).astype(F32)
    is_e = (lane >= N_EXPERT_GROUPS) & (lane < N_EXPERT_GROUPS + N_EXPERTS) & (egrp == gidx)
    ev = jnp.where(is_e, logits, ninf)
    t1 = jnp.max(ev, axis=-1, keepdims=True)
    i1 = jnp.min(jnp.where(ev == t1, lane_f, big), axis=-1, keepdims=True)
    ev2 = jnp.where(lane_f == i1, ninf, ev)
    t2 = jnp.max(ev2, axis=-1, keepdims=True)
    i2 = jnp.min(jnp.where(ev2 == t2, lane_f, big), axis=-1, keepdims=True)
    r = jnp.exp(t2 - t1)
    w1 = gprob / (1.0 + r)
    w2 = w1 * r
    return jnp.where(lane_f == i1, w1, 0.0) + jnp.where(lane_f == i2, w2, 0.0), gidx


def _merge(x_ref, oa_ref, op_ref, wo_ref, ng_ref, rw_ref, rb_ref):
    x1 = x_ref[...] + _dot(oa_ref[...], wo_ref[0:D_ATTN, :]) + _dot(op_ref[...], wo_ref[D_ATTN:, :])
    h2 = _rms(x1, ng_ref[...])
    h_hi = h2.astype(BF16)
    h_lo = (h2 - h_hi.astype(F32)).astype(BF16)
    logits = _dot(h_hi, rw_ref[0]) + _dot(h_lo, rw_ref[0]) + _dot(h_hi, rw_ref[1]) + rb_ref[...]
    comb, gidx = _route(logits)
    return x1, h2, comb, gidx


def _router_kernel(x_ref, oa_ref, op_ref, wo_ref, ng_ref, rw_ref, rb_ref, p_ref, meta_ref, cnt_ref, carry):
    i = pl.program_id(0)
    t, d = x_ref.shape

    @pl.when(i == 0)
    def _():
        carry[...] = jnp.zeros(carry.shape, F32)

    x1, _, comb, gidx = _merge(x_ref, oa_ref, op_ref, wo_ref, ng_ref, rw_ref, rb_ref)
    lane = lax.broadcasted_iota(jnp.int32, comb.shape, 1)
    onehot = jnp.where(lane.astype(F32) == gidx, 1.0, 0.0)
    r = lax.broadcasted_iota(jnp.int32, (t, t), 0)
    c = lax.broadcasted_iota(jnp.int32, (t, t), 1)
    before = jnp.where(c < r, 1.0, 0.0).astype(BF16)
    prefix = _dot(before, onehot.astype(BF16))
    rank = jnp.sum(onehot * (prefix + carry[...]), axis=-1, keepdims=True)
    carry[...] = carry[...] + jnp.sum(onehot, axis=0, keepdims=True)
    routing = comb + jnp.where(lane == 0, gidx, 0.0) + jnp.where(lane == 1, rank, 0.0)
    p_ref[:, 0:d] = x1
    p_ref[:, d:] = routing
    meta_ref[...] = routing.T[0:8, :]
    cnt_ref[...] = carry[...]


def _row_copy(src_ref, src_row, dst_ref, dst_row, sem):
    return pltpu.make_async_copy(src_ref.at[pl.ds(src_row, 1)], dst_ref.at[pl.ds(dst_row, 1)], sem)


def _scatter_rows_kernel(pos_ref, p_ref, init_ref, out_ref, sem):
    del init_ref
    t = p_ref.shape[0]

    def start(r, carry):
        _row_copy(p_ref, r, out_ref, pos_ref[0, r], sem).start()
        return carry

    def wait(r, carry):
        _row_copy(p_ref, r, out_ref, pos_ref[0, r], sem).wait()
        return carry

    lax.fori_loop(0, t, start, 0, unroll=ROW_DMA_UNROLL)
    lax.fori_loop(0, t, wait, 0, unroll=ROW_DMA_UNROLL)


def _gather_rows_kernel(pos_ref, src_ref, y_ref, sem):
    t = y_ref.shape[0]

    def start(r, carry):
        _row_copy(src_ref, pos_ref[0, r], y_ref, r, sem).start()
        return carry

    def wait(r, carry):
        _row_copy(src_ref, pos_ref[0, r], y_ref, r, sem).wait()
        return carry

    lax.fori_loop(0, t, start, 0, unroll=ROW_DMA_UNROLL)
    lax.fori_loop(0, t, wait, 0, unroll=ROW_DMA_UNROLL)


def _group_experts_kernel(cg_ref, nv_ref, p_ref, ng_ref, wg_ref, wu_ref, wd_ref, y_ref):
    c = pl.program_id(0)
    d = y_ref.shape[1]

    @pl.when(c < nv_ref[0])
    def _():
        g = cg_ref[c]
        x1 = p_ref[:, 0:d]
        routing = p_ref[:, d:]
        hb = _rms(x1, ng_ref[...]).astype(BF16)
        lane = lax.broadcasted_iota(jnp.int32, routing.shape, 1)
        y = x1
        for j in range(EXPERTS_PER_GROUP):
            gate = _dot(hb, wg_ref[j])
            a = gate * (1.0 / (1.0 + jnp.exp(-gate))) * _dot(hb, wu_ref[j])
            sel = lane == N_EXPERT_GROUPS + g * EXPERTS_PER_GROUP + j
            cj = jnp.sum(jnp.where(sel, routing, 0.0), axis=-1, keepdims=True)
            y = y + _dot((a * cj).astype(BF16), wd_ref[j])
        y_ref[...] = y

    @pl.when(c >= nv_ref[0])
    def _():
        y_ref[...] = jnp.zeros(y_ref.shape, F32)


def _routed_moe(x, oa, op, wo, ng, rw, rb, wg, wu, wd):
    n, d = x.shape
    f = wg.shape[-1]
    t, ch = MOE_ROWS, MOE_CHUNK
    w = d + ROUTER_LANES
    nt = n // t
    tile = lambda width: pl.BlockSpec((t, width), lambda i: (i, 0))
    packed, meta, counts = pl.pallas_call(
        _router_kernel,
        grid=(nt,),
        in_specs=[tile(d), tile(D_ATTN), tile(D_POOL), _full(wo.shape), _full(ng.shape), _full(rw.shape),
                  _full(rb.shape)],
        out_specs=[tile(w), pl.BlockSpec((None, 8, t), lambda i: (i, 0, 0)), _full((1, ROUTER_LANES))],
        out_shape=[jax.ShapeDtypeStruct((n, w), F32), jax.ShapeDtypeStruct((nt, 8, t), F32),
                   jax.ShapeDtypeStruct((1, ROUTER_LANES), F32)],
        scratch_shapes=[pltpu.VMEM((1, ROUTER_LANES), F32)],
        compiler_params=_params("arbitrary"),
        name="moe_router",
    )(x, oa, op, wo, ng, rw, rb)

    gid = meta[:, 0, :].astype(jnp.int32)
    rank = meta[:, 1, :].astype(jnp.int32)
    cnt = counts[0, :N_EXPERT_GROUPS].astype(jnp.int32)
    cap = (cnt + ch - 1) // ch * ch
    ends = jnp.cumsum(cap)
    base = ends - cap
    pos = rank
    for g in range(N_EXPERT_GROUPS):
        pos = pos + jnp.where(gid == g, base[g], 0)
    pos = pos.reshape(nt, 1, t)
    n_chunks = n // ch + N_EXPERT_GROUPS
    n_pad = n_chunks * ch
    chunk_group = jnp.minimum(jnp.sum(jnp.arange(n_chunks)[:, None] >= (ends // ch)[None, :], axis=1),
                              N_EXPERT_GROUPS - 1).astype(jnp.int32)
    n_valid = (ends[-1:] // ch).astype(jnp.int32)

    pos_spec = pl.BlockSpec((None, 1, t), lambda i: (i, 0, 0), memory_space=pltpu.SMEM)
    any_spec = pl.BlockSpec(memory_space=pl.ANY)
    sorted_rows = pl.pallas_call(
        _scatter_rows_kernel,
        grid=(nt,),
        in_specs=[pos_spec, tile(w), any_spec],
        out_specs=any_spec,
        out_shape=jax.ShapeDtypeStruct((n_pad, w), F32),
        scratch_shapes=[pltpu.SemaphoreType.DMA],
        input_output_aliases={2: 0},
        compiler_params=_params("arbitrary"),
        name="moe_scatter",
    )(pos, packed, jnp.zeros((n_pad, w), F32))

    wspec = lambda a, b_: pl.BlockSpec((None, EXPERTS_PER_GROUP, a, b_), lambda c, cg, nv: (cg[c], 0, 0, 0))
    clamp = lambda c, cg, nv: (jnp.minimum(c, nv[0] - 1), 0)
    y_sorted = pl.pallas_call(
        _group_experts_kernel,
        grid_spec=pltpu.PrefetchScalarGridSpec(
            num_scalar_prefetch=2,
            grid=(n_chunks,),
            in_specs=[pl.BlockSpec((ch, w), clamp), pl.BlockSpec(ng.shape, lambda c, cg, nv: (0, 0)),
                      wspec(d, f), wspec(d, f), wspec(f, d)],
            out_specs=pl.BlockSpec((ch, d), lambda c, cg, nv: (c, 0)),
        ),
        out_shape=jax.ShapeDtypeStruct((n_pad, d), F32),
        compiler_params=_params("arbitrary"),
        name="moe_experts",
    )(chunk_group, n_valid, sorted_rows, ng,
      wg.reshape(N_EXPERT_GROUPS, EXPERTS_PER_GROUP, d, f), wu.reshape(N_EXPERT_GROUPS, EXPERTS_PER_GROUP, d, f),
      wd.reshape(N_EXPERT_GROUPS, EXPERTS_PER_GROUP, f, d))

    return pl.pallas_call(
        _gather_rows_kernel,
        grid=(nt,),
        in_specs=[pos_spec, any_spec],
        out_specs=tile(d),
        out_shape=jax.ShapeDtypeStruct((n, d), F32),
        scratch_shapes=[pltpu.SemaphoreType.DMA],
        compiler_params=_params("arbitrary"),
        name="moe_gather",
    )(pos, y_sorted)


def _moe_kernel(x_ref, oa_ref, op_ref, wo_ref, ng_ref, rw_ref, rb_ref, wg_ref, wu_ref, wd_ref,
                y_ref, h2_s, comb_s, acc_s):
    e = pl.program_id(1)

    @pl.when(e == 0)
    def _():
        x1, h2, comb, _ = _merge(x_ref, oa_ref, op_ref, wo_ref, ng_ref, rw_ref, rb_ref)
        h2_s[...] = h2.astype(BF16)
        comb_s[...] = comb
        acc_s[...] = x1

    hb = h2_s[...]
    gate = _dot(hb, wg_ref[...])
    a = gate * (1.0 / (1.0 + jnp.exp(-gate))) * _dot(hb, wu_ref[...])
    lane = lax.broadcasted_iota(jnp.int32, comb_s.shape, 1)
    c = jnp.sum(jnp.where(lane == e + N_EXPERT_GROUPS, comb_s[...], 0.0), axis=-1, keepdims=True)
    acc_s[...] += _dot((a * c).astype(BF16), wd_ref[...])

    @pl.when(e == pl.num_programs(1) - 1)
    def _():
        y_ref[...] = acc_s[...]


def _params(*sem):
    return pltpu.CompilerParams(dimension_semantics=sem, vmem_limit_bytes=VMEM_LIMIT)


def _full(shape):
    return pl.BlockSpec(shape, lambda *_: (0,) * len(shape))


def _rope_tables(pos):
    inv_freq = ROPE_THETA ** (-jnp.arange(0, HEAD_DIM, 2, dtype=F32) / HEAD_DIM)
    ang = pos.astype(F32)[:, None] * inv_freq[None, :]
    cos, sin = jnp.cos(ang), jnp.sin(ang)
    reps = LANES // HEAD_DIM
    return jnp.tile(jnp.concatenate([cos, cos], 1), (1, reps)), jnp.tile(jnp.concatenate([-sin, sin], 1), (1, reps))


def _moe_call(x, oa, op, wo, ng, rw, rb, wg, wu, wd, rows):
    n, d = x.shape
    f = wg.shape[-1]
    return pl.pallas_call(
        _moe_kernel,
        grid=(n // rows, N_EXPERTS),
        in_specs=[
            pl.BlockSpec((rows, d), lambda i, e: (i, 0)),
            pl.BlockSpec((rows, D_ATTN), lambda i, e: (i, 0)),
            pl.BlockSpec((rows, D_POOL), lambda i, e: (i, 0)),
            _full(wo.shape), _full(ng.shape), _full(rw.shape), _full(rb.shape),
            pl.BlockSpec((None, d, f), lambda i, e: (e, 0, 0)),
            pl.BlockSpec((None, d, f), lambda i, e: (e, 0, 0)),
            pl.BlockSpec((None, f, d), lambda i, e: (e, 0, 0)),
        ],
        out_specs=pl.BlockSpec((rows, d), lambda i, e: (i, 0)),
        out_shape=jax.ShapeDtypeStruct((n, d), F32),
        scratch_shapes=[pltpu.VMEM((rows, d), BF16), pltpu.VMEM((rows, ROUTER_LANES), F32), pltpu.VMEM((rows, d), F32)],
        compiler_params=_params("parallel", "arbitrary"),
        name="merge_moe",
    )(x, oa, op, wo, ng, rw, rb, wg, wu, wd)


def kernel(x_prompt, x_sample, cache_k, cache_v, state_pool, page_table, meta_tokens, norm_attn_g, w_in, q_norm_g, k_norm_g, lambda_q1, lambda_k1, lambda_q2, lambda_k2, subln_g, pool_w, pool_scale, w_out, norm_ffn_g, router_group_w, router_group_b, router_expert_w, router_expert_b, expert_w_gate, expert_w_up, expert_w_down):
    B, SEQ, D = x_prompt.shape
    Bd, S_dec, _ = x_sample.shape
    n_phys, page = cache_k.shape[1], cache_k.shape[2]
    n_pages = page_table.shape[1]
    past_len = n_pages * page
    assert cache_k.shape[0] == 1 and N_META == MAX_WINDOW

    ng = norm_attn_g[0][None, :]
    win = w_in[0].astype(BF16)
    reps = D_QK // HEAD_DIM
    qg = jnp.tile(q_norm_g[0], reps)[None, :]
    kg = jnp.tile(k_norm_g[0], reps)[None, :]
    grp = jnp.arange(D_QK) // HEAD_DIM
    gmat = jnp.where(grp[:, None] == grp[None, :], 1.0 / HEAD_DIM, 0.0).astype(BF16)
    pw = pool_w[0].astype(BF16)
    ps = pool_scale[0][None, :]
    lq1, lk1, lq2, lk2 = (a[0][None, :] for a in (lambda_q1, lambda_k1, lambda_q2, lambda_k2))
    sg = subln_g[0][None, :]
    wo = w_out[0].astype(BF16)
    nfg = norm_ffn_g[0][None, :]
    pad = ROUTER_LANES - N_EXPERT_GROUPS - N_EXPERTS
    rw = jnp.concatenate([router_group_w[0], router_expert_w[0], jnp.zeros((D, pad), F32)], axis=1)
    rw_hi = rw.astype(BF16)
    rw = jnp.stack([rw_hi, (rw - rw_hi.astype(F32)).astype(BF16)])
    rb = jnp.concatenate([router_group_b[0], router_expert_b[0], jnp.zeros((pad,), F32)])[None, :]
    wg = expert_w_gate[0].astype(BF16)
    wu = expert_w_up[0].astype(BF16)
    wd = expert_w_down[0].astype(BF16)

    n_s = Bd * S_dec
    n_small = n_s + N_META
    xs_t = jnp.transpose(x_sample, (1, 0, 2)).reshape(n_s, D)
    x_small = jnp.concatenate([xs_t, meta_tokens.astype(F32)], axis=0)
    pos_small = jnp.concatenate([jnp.repeat(past_len + jnp.arange(S_dec), Bd), jnp.arange(N_META)])
    cos_s, sin_s = _rope_tables(pos_small)
    st_t = jnp.transpose(state_pool[0], (1, 0, 2))
    q_s, k_s, v_s, u_s, opool_s = pl.pallas_call(
        functools.partial(_proj_small_kernel, n_dec=Bd, s_dec=S_dec),
        out_shape=[jax.ShapeDtypeStruct((n_small, D_QK), F32)] * 4 + [jax.ShapeDtypeStruct((n_s, D_POOL), BF16)],
        compiler_params=pltpu.CompilerParams(vmem_limit_bytes=VMEM_LIMIT),
        name="proj_small",
    )(x_small, cos_s, sin_s, st_t, ng, win, gmat, qg, kg, pw, ps)
    k_meta, v_meta, u_meta = k_s[n_s:], v_s[n_s:], u_s[n_s:]

    tq = min(ATTN_ROWS, SEQ)
    tp = tq
    cos_p, sin_p = _rope_tables(N_META + jnp.arange(SEQ))
    row_spec = lambda w: pl.BlockSpec((None, tp, w), lambda b, i: (b, i, 0))
    tab_spec = pl.BlockSpec((tp, LANES), lambda b, i: (i, 0))
    vt_spec = pl.BlockSpec((None, N_HEADS, None, V_DIM, tp), lambda b, i: (b, 0, i, 0, 0))
    q_p, kb_p, vt_p, kf_p, vf_p, opool_p, utail_p = pl.pallas_call(
        _proj_prompt_kernel,
        grid=(B, SEQ // tp),
        in_specs=[row_spec(D), tab_spec, tab_spec, _full(u_meta.shape), _full(ng.shape), _full(win.shape),
                  _full(gmat.shape), _full(qg.shape), _full(kg.shape), _full(pw.shape), _full(ps.shape)],
        out_specs=[row_spec(D_QK), row_spec(D_QK), vt_spec, row_spec(D_QK), row_spec(D_ATTN), row_spec(D_POOL),
                   pl.BlockSpec((None, MAX_WINDOW, D_POOL), lambda b, i: (b, 0, 0))],
        out_shape=[jax.ShapeDtypeStruct((B, SEQ, D_QK), BF16)] * 2
        + [jax.ShapeDtypeStruct((B, N_HEADS, SEQ // tp, V_DIM, tp), BF16)]
        + [jax.ShapeDtypeStruct((B, SEQ, D_QK), F32)] * 2
        + [jax.ShapeDtypeStruct((B, SEQ, D_POOL), BF16), jax.ShapeDtypeStruct((B, MAX_WINDOW, D_POOL), F32)],
        scratch_shapes=[pltpu.VMEM((MAX_WINDOW + tp, D_POOL), F32)],
        compiler_params=_params("parallel", "arbitrary"),
        name="proj_prompt",
    )(x_prompt, cos_p, sin_p, u_meta, ng, win, gmat, qg, kg, pw, ps)

    km_pad = jnp.zeros((LANES, D_QK), BF16).at[:N_META].set(k_meta.astype(BF16))
    vm_pad = jnp.zeros((LANES, D_ATTN), BF16).at[:N_META].set(v_meta.astype(BF16))
    vmt_pad = jnp.transpose(vm_pad.reshape(LANES, N_HEADS, V_DIM), (1, 2, 0))
    vec_spec3 = lambda w: pl.BlockSpec((1, w), lambda b, h, i: (0, 0))
    oatt_p = pl.pallas_call(
        _attn_prompt_kernel,
        grid=(B, N_HEADS, SEQ // tq),
        in_specs=[
            pl.BlockSpec((None, tq, V_DIM), lambda b, h, i: (b, i, h)),
            pl.BlockSpec((None, SEQ, V_DIM), lambda b, h, i: (b, 0, h)),
            pl.BlockSpec((None, None, SEQ // tq, V_DIM, tq), lambda b, h, i: (b, h, 0, 0, 0)),
            pl.BlockSpec((LANES, V_DIM), lambda b, h, i: (0, h)),
            pl.BlockSpec((None, V_DIM, LANES), lambda b, h, i: (h, 0, 0)),
            vec_spec3(HEAD_DIM), vec_spec3(HEAD_DIM), vec_spec3(HEAD_DIM), vec_spec3(HEAD_DIM),
            pl.BlockSpec((V_DIM, 1), lambda b, h, i: (0, 0)),
        ],
        out_specs=pl.BlockSpec((None, tq, V_DIM), lambda b, h, i: (b, i, h)),
        out_shape=jax.ShapeDtypeStruct((B, SEQ, D_ATTN), BF16),
        scratch_shapes=[pltpu.VMEM((2 * tq, V_DIM), BF16), pltpu.VMEM((1, 2 * tq), F32), pltpu.VMEM((1, 2 * tq), F32),
                        pltpu.VMEM((V_DIM, 2 * tq), F32), pltpu.VMEM((tq, 2 * tq), F32), pltpu.VMEM((tq, 2 * tq), F32)],
        compiler_params=_params("parallel", "parallel", "arbitrary"),
        name="attn_prompt",
    )(q_p, kb_p, vt_p, km_pad, vmt_pad, lq1, lk1, lq2, lk2, sg.reshape(V_DIM, 1))

    hs = N_HEADS * S_dec
    qd = q_s[:n_s].reshape(S_dec, Bd, N_HEADS, 2, HEAD_DIM)
    qd = jnp.transpose(qd, (1, 3, 2, 0, 4))
    qmat = jnp.zeros((Bd, 2, hs, 2, HEAD_DIM), F32)
    qmat = qmat.at[:, 0, :, 0].set(qd[:, 0].reshape(Bd, hs, HEAD_DIM)).at[:, 1, :, 1].set(qd[:, 1].reshape(Bd, hs, HEAD_DIM))
    qmat = qmat.reshape(Bd, 2 * hs, V_DIM).astype(BF16)
    new_rows = lambda a: jnp.transpose(a[:n_s].reshape(S_dec, Bd, hs // S_dec * V_DIM), (1, 0, 2)).reshape(Bd, hs, V_DIM)
    kn_pad = jnp.zeros((Bd, LANES, V_DIM), BF16).at[:, :hs].set(new_rows(k_s).astype(BF16))
    vn_pad = jnp.zeros((Bd, LANES, V_DIM), BF16).at[:, :hs].set(new_rows(v_s).astype(BF16))
    ck = cache_k.reshape(n_phys, page * N_HEADS, V_DIM)
    cv = cache_v.reshape(n_phys, page * N_HEADS, V_DIM)
    n_pg = math.gcd(DECODE_PAGES, n_pages)
    page_spec = lambda j: pl.BlockSpec((None, page * N_HEADS, V_DIM), lambda b, g, pt: (pt[b, g * n_pg + j], 0, 0))
    vec_spec2 = lambda w: pl.BlockSpec((1, w), lambda b, g, pt: (0, 0))
    per_b = lambda r: pl.BlockSpec((None, r, V_DIM), lambda b, g, pt: (b, 0, 0))
    oatt_s = pl.pallas_call(
        functools.partial(_decode_kernel, n_pg=n_pg, s_dec=S_dec),
        grid_spec=pltpu.PrefetchScalarGridSpec(
            num_scalar_prefetch=1,
            grid=(Bd, n_pages // n_pg),
            in_specs=[per_b(2 * hs)] + [page_spec(j) for j in range(n_pg)] * 2 + [per_b(LANES), per_b(LANES)]
            + [vec_spec2(HEAD_DIM)] * 4 + [vec_spec2(V_DIM)],
            out_specs=per_b(hs),
            scratch_shapes=[pltpu.VMEM((2 * hs, 1), F32), pltpu.VMEM((2 * hs, 1), F32), pltpu.VMEM((2 * hs, V_DIM), F32)],
        ),
        out_shape=jax.ShapeDtypeStruct((Bd, hs, V_DIM), BF16),
        compiler_params=_params("parallel", "arbitrary"),
        name="attn_decode",
    )(page_table, qmat, *([ck] * n_pg), *([cv] * n_pg), kn_pad, vn_pad, lq1, lk1, lq2, lk2, sg)
    oatt_s = jnp.transpose(oatt_s.reshape(Bd, N_HEADS, S_dec, V_DIM), (2, 0, 1, 3)).reshape(n_s, D_ATTN)

    y_p = _routed_moe(x_prompt.reshape(B * SEQ, D), oatt_p.reshape(B * SEQ, D_ATTN), opool_p.reshape(B * SEQ, D_POOL),
                      wo, nfg, rw, rb, wg, wu, wd)
    y_s = _moe_call(xs_t, oatt_s, opool_s, wo, nfg, rw, rb, wg, wu, wd, min(MOE_ROWS, n_s))

    y_prompt = y_p.reshape(B, SEQ, D)
    y_sample = jnp.transpose(y_s.reshape(S_dec, Bd, D), (1, 0, 2))
    bc = lambda a: jnp.broadcast_to(a[None], (B,) + a.shape)
    new_k_prompt = jnp.concatenate([bc(k_meta), kf_p], axis=1).reshape(1, B, N_META + SEQ, N_HEADS, V_DIM)
    new_v_prompt = jnp.concatenate([bc(v_meta), vf_p], axis=1).reshape(1, B, N_META + SEQ, N_HEADS, V_DIM)
    new_pool_prompt = utail_p[:, MAX_WINDOW - POOL_STATE:][None]
    to_b = lambda a, w: jnp.transpose(a[:n_s].reshape(S_dec, Bd, w), (1, 0, 2))
    new_k_sample = to_b(k_s, D_QK).reshape(1, Bd, S_dec, N_HEADS, V_DIM)
    new_v_sample = to_b(v_s, D_ATTN).reshape(1, Bd, S_dec, N_HEADS, V_DIM)
    u_new = to_b(u_s, D_POOL)
    new_pool_sample = jnp.concatenate([state_pool[0].astype(F32), u_new], axis=1)[:, S_dec:][None]
    return (y_prompt, y_sample, new_k_prompt, new_v_prompt, new_pool_prompt, new_k_sample, new_v_sample,
            new_pool_sample)
```

```python
import functools
import math

import jax
import jax.numpy as jnp
from jax import lax
from jax.experimental import pallas as pl
from jax.experimental.pallas import tpu as pltpu

N_META = 16
N_HEADS = 4
HEAD_DIM = 64
V_DIM = 2 * HEAD_DIM
D_QK = N_HEADS * 2 * HEAD_DIM
D_ATTN = N_HEADS * V_DIM
POOL_WINDOWS = (2, 4, 8, 16)
POOL_GROUP = 128
MAX_WINDOW = max(POOL_WINDOWS)
POOL_STATE = MAX_WINDOW - 1
D_POOL = POOL_GROUP * len(POOL_WINDOWS)
N_EXPERT_GROUPS = 4
EXPERTS_PER_GROUP = 4
N_EXPERTS = N_EXPERT_GROUPS * EXPERTS_PER_GROUP
ROPE_THETA = 10000.0
EPS = 1e-6
NEG = -1e30
LOG2E = 1.4426950408889634
LAM_INIT = 0.8 - 0.6 * math.exp(-0.3 * 0)
Q_SCALE = HEAD_DIM ** -0.5 * LOG2E

LANES = 128
ROUTER_LANES = LANES
VMEM_LIMIT = 48 * 1024 * 1024

ATTN_ROWS = 512
ATTN_LANE_BLOCK = 256
MOE_ROWS = 512
MOE_CHUNK = 512
ROW_DMA_UNROLL = 8
DECODE_PAGES = 8
DECODE_SLOTS = 3

F32 = jnp.float32
BF16 = jnp.bfloat16


def _dot(a, b):
    return jnp.dot(a, b, preferred_element_type=F32)


def _dot_nt(a, b):
    return lax.dot_general(a, b, (((1,), (1,)), ((), ())), preferred_element_type=F32)


def _rms(x, g):
    return x * lax.rsqrt(jnp.mean(x * x, axis=-1, keepdims=True) + EPS) * g


def _qk_norm_rope(z, gmat, g_t, cos_t, sin_t):
    ms = _dot((z * z).astype(BF16), gmat)
    xn = z * lax.rsqrt(ms + EPS) * g_t
    outs = []
    for j in range(z.shape[1] // LANES):
        xb = xn[:, j * LANES:(j + 1) * LANES]
        lane = lax.broadcasted_iota(jnp.int32, xb.shape, 1)
        first_half = (lane % HEAD_DIM) < (HEAD_DIM // 2)
        rot = jnp.where(first_half, pltpu.roll(xb, LANES - HEAD_DIM // 2, 1), pltpu.roll(xb, HEAD_DIM // 2, 1))
        outs.append(xb * cos_t + rot * sin_t)
    return jnp.concatenate(outs, axis=1)


def _project(x, ng, win_ref, gmat, qg, kg, cos_t, sin_t):
    h = _rms(x, ng).astype(BF16)
    q = _qk_norm_rope(_dot(h, win_ref[:, 0:D_QK]), gmat, qg, cos_t, sin_t) * Q_SCALE
    k = _qk_norm_rope(_dot(h, win_ref[:, D_QK:2 * D_QK]), gmat, kg, cos_t, sin_t)
    v = _dot(h, win_ref[:, 2 * D_QK:2 * D_QK + D_ATTN])
    u = _dot(h, win_ref[:, 2 * D_QK + D_ATTN:])
    return q, k, v, u


def _proj_small_kernel(x_ref, cos_ref, sin_ref, st_ref, ng_ref, win_ref, gmat_ref, qg_ref, kg_ref, pw_ref, ps_ref,
                       q_ref, k_ref, v_ref, u_ref, op_ref, *, n_dec, s_dec):
    q, k, v, u = _project(x_ref[...], ng_ref[...], win_ref, gmat_ref[...], qg_ref[...], kg_ref[...],
                          cos_ref[...], sin_ref[...])
    q_ref[...] = q
    k_ref[...] = k
    v_ref[...] = v
    u_ref[...] = u
    for g, w in enumerate(POOL_WINDOWS):
        cs = slice(g * POOL_GROUP, (g + 1) * POOL_GROUP)
        us = [u[s * n_dec:(s + 1) * n_dec, cs] for s in range(s_dec)]
        need = {w - 1 - s for s in range(s_dec) if w - 1 - s > 0}
        suffix = {0: jnp.zeros_like(us[0])}
        run = jnp.zeros_like(us[0])
        for c in range(1, max(need) + 1 if need else 1):
            run = run + st_ref[POOL_STATE - c, :, cs]
            if c in need:
                suffix[c] = run
        for s in range(s_dec):
            win = suffix[max(w - 1 - s, 0)]
            for sp in range(max(0, s - w + 1), s + 1):
                win = win + us[sp]
            d = win * (1.0 / w) - us[s]
            o = _dot(d.astype(BF16), pw_ref[g]) * ps_ref[:, cs]
            op_ref[s * n_dec:(s + 1) * n_dec, cs] = o.astype(BF16)


def _proj_prompt_kernel(x_ref, cos_ref, sin_ref, um_ref, ng_ref, win_ref, gmat_ref, qg_ref, kg_ref, pw_ref, ps_ref,
                        q_ref, kb_ref, vt_ref, kf_ref, vf_ref, op_ref, ut_ref, ubuf):
    i = pl.program_id(1)
    t = x_ref.shape[0]
    q, k, v, u = _project(x_ref[...], ng_ref[...], win_ref, gmat_ref[...], qg_ref[...], kg_ref[...],
                          cos_ref[...], sin_ref[...])
    q_ref[...] = q.astype(BF16)
    kf_ref[...] = k
    kb_ref[...] = k.astype(BF16)
    vf_ref[...] = v
    for h in range(N_HEADS):
        vt_ref[h] = v[:, h * V_DIM:(h + 1) * V_DIM].T.astype(BF16)

    @pl.when(i == 0)
    def _():
        ubuf[0:MAX_WINDOW, :] = um_ref[...]

    ubuf[MAX_WINDOW:MAX_WINDOW + t, :] = u
    for g, w in enumerate(POOL_WINDOWS):
        cs = slice(g * POOL_GROUP, (g + 1) * POOL_GROUP)
        e = ubuf[:, cs]
        a = e + pltpu.roll(e, 1, 0)
        sh = 2
        while sh < w:
            a = a + pltpu.roll(a, sh, 0)
            sh *= 2
        d = a[MAX_WINDOW:, :] * (1.0 / w) - e[MAX_WINDOW:, :]
        o = _dot(d.astype(BF16), pw_ref[g]) * ps_ref[:, cs]
        op_ref[:, cs] = o.astype(BF16)
    tail = ubuf[t:t + MAX_WINDOW, :]
    ut_ref[...] = tail
    ubuf[0:MAX_WINDOW, :] = tail


def _diff_lambda(lq1_ref, lk1_ref, lq2_ref, lk2_ref):
    a = jnp.sum(lq1_ref[...] * lk1_ref[...], axis=-1, keepdims=True)
    b = jnp.sum(lq2_ref[...] * lk2_ref[...], axis=-1, keepdims=True)
    return jnp.exp(a) - jnp.exp(b) + LAM_INIT


def _online_update(s, v, m_s, l_s, acc_s):
    m_old = m_s[...]
    m_new = jnp.maximum(m_old, jnp.max(s, axis=-1, keepdims=True))
    alpha = jnp.exp2(m_old - m_new)
    p = jnp.exp2(s - m_new)
    l_s[...] = alpha * l_s[...] + jnp.sum(p, axis=-1, keepdims=True)
    acc_s[...] = alpha * acc_s[...] + _dot(p.astype(BF16), v)
    m_s[...] = m_new


def _diff_finalize(n, lam, sg, l_s, acc_s):
    o = acc_s[0:n, :] / l_s[0:n, :] - lam * (acc_s[n:2 * n, :] / l_s[n:2 * n, :])
    return _rms(o, sg) * (1.0 - LAM_INIT)


def _attn_prompt_kernel(q_ref, k_ref, vt_ref, km_ref, vmt_ref, lq1_ref, lk1_ref, lq2_ref, lk2_ref, sgc_ref,
                        o_ref, qs, m_s, l_s, acc_s, sa, sb):
    qi = pl.program_id(2)
    tq = q_ref.shape[0]
    q = q_ref[...]
    lane = lax.broadcasted_iota(jnp.int32, q.shape, 1)
    zero = jnp.zeros_like(q)
    qs[0:tq, :] = jnp.where(lane < HEAD_DIM, q, zero)
    qs[tq:2 * tq, :] = jnp.where(lane >= HEAD_DIM, q, zero)
    n_blk = 2 * tq // ATTN_LANE_BLOCK

    blocks = [slice(j * ATTN_LANE_BLOCK, (j + 1) * ATTN_LANE_BLOCK) for j in range(n_blk)]
    m_s[...] = jnp.full(m_s.shape, NEG, F32)
    l_s[...] = jnp.zeros(l_s.shape, F32)
    acc_s[...] = jnp.zeros(acc_s.shape, F32)

    def scores_to(kk, buf):
        k = k_ref[pl.ds(pl.multiple_of(kk * tq, tq), tq), :]
        for cs in blocks:
            buf[:, cs] = _dot_nt(k, qs[cs, :])

    def update(cs, parts):
        m_old = m_s[:, cs]
        m_new = m_old
        for s, _ in parts:
            m_new = jnp.maximum(m_new, jnp.max(s, axis=0, keepdims=True))
        alpha = jnp.exp2(m_old - m_new)
        l = alpha * l_s[:, cs]
        acc = alpha * acc_s[:, cs]
        for s, vt in parts:
            p = jnp.exp2(s - m_new)
            l = l + jnp.sum(p, axis=0, keepdims=True)
            acc = acc + _dot(vt, p.astype(BF16))
        l_s[:, cs] = l
        acc_s[:, cs] = acc
        m_s[:, cs] = m_new

    def process(buf, kk):
        vt = vt_ref[kk]
        for cs in blocks:
            update(cs, [(buf[:, cs], vt)])

    def last_step(buf):
        vt = vt_ref[qi]
        for j, cs in enumerate(blocks):
            s_meta = _dot_nt(km_ref[...], qs[cs, :])
            mkey = lax.broadcasted_iota(jnp.int32, s_meta.shape, 0)
            s_meta = jnp.where(mkey < N_META, s_meta, NEG)
            s = buf[:, cs]
            key = lax.broadcasted_iota(jnp.int32, s.shape, 0)
            ql = lax.broadcasted_iota(jnp.int32, s.shape, 1) + j * ATTN_LANE_BLOCK
            tok = jnp.where(ql >= tq, ql - tq, ql)
            s = jnp.where(key <= tok, s, NEG)
            update(cs, [(s_meta, vmt_ref[...]), (s, vt)])

    scores_to(0, sa)

    def pair(pp, carry):
        k0 = 2 * pp
        scores_to(k0 + 1, sb)
        process(sa, k0)
        scores_to(k0 + 2, sa)
        process(sb, k0 + 1)
        return carry

    lax.fori_loop(0, qi // 2, pair, 0)

    @pl.when(qi % 2 == 1)
    def _():
        scores_to(qi, sb)
        process(sa, qi - 1)
        last_step(sb)

    @pl.when(qi % 2 == 0)
    def _():
        last_step(sa)

    lam = _diff_lambda(lq1_ref, lk1_ref, lq2_ref, lk2_ref)
    o = acc_s[:, 0:tq] / l_s[:, 0:tq] - lam * (acc_s[:, tq:2 * tq] / l_s[:, tq:2 * tq])
    on = o * lax.rsqrt(jnp.mean(o * o, axis=0, keepdims=True) + EPS) * sgc_ref[...] * (1.0 - LAM_INIT)
    o_ref[...] = on.T.astype(BF16)


def _decode_kernel(pt_ref, q_ref, kn_ref, vn_ref, lq1_ref, lk1_ref, lq2_ref, lk2_ref, sg_ref, ck_ref, cv_ref,
                   o_ref, kbuf, vbuf, sems, m_s, l_s, acc_s, *, n_pg, s_dec):
    b = pl.program_id(0)
    g = pl.program_id(1)
    n_g = pl.num_programs(1)
    step = b * n_g + g
    n_steps = pl.num_programs(0) * n_g
    n_rows = q_ref.shape[0]
    half = n_rows // 2

    def page_copies(st, slot):
        bb = st // n_g
        g0 = (st % n_g) * n_pg
        copies = []
        for j in range(n_pg):
            pid = pt_ref[bb, g0 + j]
            copies.append(pltpu.make_async_copy(ck_ref.at[pid], kbuf.at[slot, j], sems.at[slot, 0]))
            copies.append(pltpu.make_async_copy(cv_ref.at[pid], vbuf.at[slot, j], sems.at[slot, 1]))
        return copies

    def start_step(st):
        for cp in page_copies(st, st % DECODE_SLOTS):
            cp.start()

    @pl.when(step == 0)
    def _():
        for d in range(DECODE_SLOTS - 1):
            @pl.when(d < n_steps)
            def _():
                start_step(d)

    @pl.when(step + DECODE_SLOTS - 1 < n_steps)
    def _():
        start_step(step + DECODE_SLOTS - 1)

    @pl.when(g == 0)
    def _():
        m_s[...] = jnp.full(m_s.shape, NEG, F32)
        l_s[...] = jnp.zeros(l_s.shape, F32)
        acc_s[...] = jnp.zeros(acc_s.shape, F32)

    slot = step % DECODE_SLOTS
    for cp in page_copies(step, slot):
        cp.wait()
    k_refs = [kbuf.at[slot, j] for j in range(n_pg)]
    v_refs = [vbuf.at[slot, j] for j in range(n_pg)]

    q = q_ref[...]
    page_cols = kbuf.shape[2]
    row = lax.broadcasted_iota(jnp.int32, (n_rows, page_cols), 0)
    col = lax.broadcasted_iota(jnp.int32, (n_rows, page_cols), 1)
    head_ok = (col % N_HEADS) == ((row % half) // s_dec)
    ss = []
    for j in range(n_pg):
        s = _dot_nt(q, k_refs[j][...].astype(BF16))
        ss.append(jnp.where(head_ok, s, NEG))
    m_old = m_s[...]
    m_new = m_old
    for s in ss:
        m_new = jnp.maximum(m_new, jnp.max(s, axis=-1, keepdims=True))
    alpha = jnp.exp2(m_old - m_new)
    l = alpha * l_s[...]
    acc = alpha * acc_s[...]
    for j in range(n_pg):
        p = jnp.exp2(ss[j] - m_new)
        l = l + jnp.sum(p, axis=-1, keepdims=True)
        acc = acc + _dot(p.astype(BF16), v_refs[j][...].astype(BF16))
    m_s[...] = m_new
    l_s[...] = l
    acc_s[...] = acc

    @pl.when(g == pl.num_programs(1) - 1)
    def _():
        s = _dot_nt(q, kn_ref[...])
        r2 = lax.broadcasted_iota(jnp.int32, s.shape, 0)
        c2 = lax.broadcasted_iota(jnp.int32, s.shape, 1)
        ok = (c2 < s_dec * N_HEADS) & ((c2 % N_HEADS) == ((r2 % half) // s_dec)) & ((c2 // N_HEADS) <= (r2 % s_dec))
        _online_update(jnp.where(ok, s, NEG), vn_ref[...], m_s, l_s, acc_s)
        lam = _diff_lambda(lq1_ref, lk1_ref, lq2_ref, lk2_ref)
        o_ref[...] = _diff_finalize(half, lam, sg_ref[...], l_s, acc_s).astype(BF16)


def _route(logits):
    ninf = float("-inf")
    lane = lax.broadcasted_iota(jnp.int32, logits.shape, 1)
    lane_f = lane.astype(F32)
    big = float(ROUTER_LANES)
    is_g = lane < N_EXPERT_GROUPS
    gl = jnp.where(is_g, logits, ninf)
    gmax = jnp.max(gl, axis=-1, keepdims=True)
    gidx = jnp.min(jnp.where(gl == gmax, lane_f, big), axis=-1, keepdims=True)
    gsum = jnp.sum(jnp.where(is_g, jnp.exp(logits - gmax), 0.0), axis=-1, keepdims=True)
    gprob = 1.0 / gsum
    egrp = ((lane - N_EXPERT_GROUPS) // EXPERTS_PER_GROUP).astype(F32)
    is_e = (lane >= N_EXPERT_GROUPS) & (lane < N_EXPERT_GROUPS + N_EXPERTS) & (egrp == gidx)
    ev = jnp.where(is_e, logits, ninf)
    t1 = jnp.max(ev, axis=-1, keepdims=True)
    i1 = jnp.min(jnp.where(ev == t1, lane_f, big), axis=-1, keepdims=True)
    ev2 = jnp.where(lane_f == i1, ninf, ev)
    t2 = jnp.max(ev2, axis=-1, keepdims=True)
    i2 = jnp.min(jnp.where(ev2 == t2, lane_f, big), axis=-1, keepdims=True)
    r = jnp.exp(t2 - t1)
    w1 = gprob / (1.0 + r)
    w2 = w1 * r
    return jnp.where(lane_f == i1, w1, 0.0) + jnp.where(lane_f == i2, w2, 0.0), gidx


def _merge(x_ref, oa_ref, op_ref, wo_ref, ng_ref, rw_ref, rb_ref):
    x1 = x_ref[...] + _dot(oa_ref[...], wo_ref[0:D_ATTN, :]) + _dot(op_ref[...], wo_ref[D_ATTN:, :])
    h2 = _rms(x1, ng_ref[...])
    h_hi = h2.astype(BF16)
    h_lo = (h2 - h_hi.astype(F32)).astype(BF16)
    logits = _dot(h_hi, rw_ref[0]) + _dot(h_lo, rw_ref[0]) + _dot(h_hi, rw_ref[1]) + rb_ref[...]
    comb, gidx = _route(logits)
    return x1, h2, comb, gidx


def _router_kernel(x_ref, oa_ref, op_ref, wo_ref, ng_ref, rw_ref, rb_ref, p_ref, meta_ref, cnt_ref, carry):
    i = pl.program_id(0)
    t, d = x_ref.shape

    @pl.when(i == 0)
    def _():
        carry[...] = jnp.zeros(carry.shape, F32)

    x1, _, comb, gidx = _merge(x_ref, oa_ref, op_ref, wo_ref, ng_ref, rw_ref, rb_ref)
    lane = lax.broadcasted_iota(jnp.int32, comb.shape, 1)
    onehot = jnp.where(lane.astype(F32) == gidx, 1.0, 0.0)
    r = lax.broadcasted_iota(jnp.int32, (t, t), 0)
    c = lax.broadcasted_iota(jnp.int32, (t, t), 1)
    before = jnp.where(c < r, 1.0, 0.0).astype(BF16)
    prefix = _dot(before, onehot.astype(BF16))
    rank = jnp.sum(onehot * (prefix + carry[...]), axis=-1, keepdims=True)
    carry[...] = carry[...] + jnp.sum(onehot, axis=0, keepdims=True)
    routing = comb + jnp.where(lane == 0, gidx, 0.0) + jnp.where(lane == 1, rank, 0.0)
    p_ref[:, 0:d] = x1
    p_ref[:, d:] = routing
    meta_ref[...] = routing.T[0:8, :]
    cnt_ref[...] = carry[...]


def _row_copy(src_ref, src_row, dst_ref, dst_row, sem):
    return pltpu.make_async_copy(src_ref.at[pl.ds(src_row, 1)], dst_ref.at[pl.ds(dst_row, 1)], sem)


def _scatter_rows_kernel(pos_ref, p_ref, init_ref, out_ref, sem):
    del init_ref
    t = p_ref.shape[0]

    def start(r, carry):
        _row_copy(p_ref, r, out_ref, pos_ref[0, r], sem).start()
        return carry

    def wait(r, carry):
        _row_copy(p_ref, r, out_ref, pos_ref[0, r], sem).wait()
        return carry

    lax.fori_loop(0, t, start, 0, unroll=ROW_DMA_UNROLL)
    lax.fori_loop(0, t, wait, 0, unroll=ROW_DMA_UNROLL)


def _gather_rows_kernel(pos_ref, src_ref, y_ref, sem):
    t = y_ref.shape[0]

    def start(r, carry):
        _row_copy(src_ref, pos_ref[0, r], y_ref, r, sem).start()
        return carry

    def wait(r, carry):
        _row_copy(src_ref, pos_ref[0, r], y_ref, r, sem).wait()
        return carry

    lax.fori_loop(0, t, start, 0, unroll=ROW_DMA_UNROLL)
    lax.fori_loop(0, t, wait, 0, unroll=ROW_DMA_UNROLL)


def _group_experts_kernel(cg_ref, nv_ref, p_ref, ng_ref, wg_ref, wu_ref, wd_ref, y_ref):
    c = pl.program_id(0)
    d = y_ref.shape[1]

    @pl.when(c < nv_ref[0])
    def _():
        g = cg_ref[c]
        x1 = p_ref[:, 0:d]
        routing = p_ref[:, d:]
        hb = _rms(x1, ng_ref[...]).astype(BF16)
        lane = lax.broadcasted_iota(jnp.int32, routing.shape, 1)
        y = x1
        for j in range(EXPERTS_PER_GROUP):
            gate = _dot(hb, wg_ref[j])
            a = gate * (1.0 / (1.0 + jnp.exp(-gate))) * _dot(hb, wu_ref[j])
            sel = lane == N_EXPERT_GROUPS + g * EXPERTS_PER_GROUP + j
            cj = jnp.sum(jnp.where(sel, routing, 0.0), axis=-1, keepdims=True)
            y = y + _dot((a * cj).astype(BF16), wd_ref[j])
        y_ref[...] = y

    @pl.when(c >= nv_ref[0])
    def _():
        y_ref[...] = jnp.zeros(y_ref.shape, F32)


def _routed_moe(x, oa, op, wo, ng, rw, rb, wg, wu, wd):
    n, d = x.shape
    f = wg.shape[-1]
    t, ch = MOE_ROWS, MOE_CHUNK
    w = d + ROUTER_LANES
    nt = n // t
    tile = lambda width: pl.BlockSpec((t, width), lambda i: (i, 0))
    packed, meta, counts = pl.pallas_call(
        _router_kernel,
        grid=(nt,),
        in_specs=[tile(d), tile(D_ATTN), tile(D_POOL), _full(wo.shape), _full(ng.shape), _full(rw.shape),
                  _full(rb.shape)],
        out_specs=[tile(w), pl.BlockSpec((None, 8, t), lambda i: (i, 0, 0)), _full((1, ROUTER_LANES))],
        out_shape=[jax.ShapeDtypeStruct((n, w), F32), jax.ShapeDtypeStruct((nt, 8, t), F32),
                   jax.ShapeDtypeStruct((1, ROUTER_LANES), F32)],
        scratch_shapes=[pltpu.VMEM((1, ROUTER_LANES), F32)],
        compiler_params=_params("arbitrary"),
        name="moe_router",
    )(x, oa, op, wo, ng, rw, rb)

    gid = meta[:, 0, :].astype(jnp.int32)
    rank = meta[:, 1, :].astype(jnp.int32)
    cnt = counts[0, :N_EXPERT_GROUPS].astype(jnp.int32)
    cap = (cnt + ch - 1) // ch * ch
    ends = jnp.cumsum(cap)
    base = ends - cap
    pos = rank
    for g in range(N_EXPERT_GROUPS):
        pos = pos + jnp.where(gid == g, base[g], 0)
    pos = pos.reshape(nt, 1, t)
    n_chunks = n // ch + N_EXPERT_GROUPS
    n_pad = n_chunks * ch
    chunk_group = jnp.minimum(jnp.sum(jnp.arange(n_chunks)[:, None] >= (ends // ch)[None, :], axis=1),
                              N_EXPERT_GROUPS - 1).astype(jnp.int32)
    n_valid = (ends[-1:] // ch).astype(jnp.int32)

    pos_spec = pl.BlockSpec((None, 1, t), lambda i: (i, 0, 0), memory_space=pltpu.SMEM)
    any_spec = pl.BlockSpec(memory_space=pl.ANY)
    sorted_rows = pl.pallas_call(
        _scatter_rows_kernel,
        grid=(nt,),
        in_specs=[pos_spec, tile(w), any_spec],
        out_specs=any_spec,
        out_shape=jax.ShapeDtypeStruct((n_pad, w), F32),
        scratch_shapes=[pltpu.SemaphoreType.DMA],
        input_output_aliases={2: 0},
        compiler_params=_params("arbitrary"),
        name="moe_scatter",
    )(pos, packed, jnp.zeros((n_pad, w), F32))

    wspec = lambda a, b_: pl.BlockSpec((None, EXPERTS_PER_GROUP, a, b_), lambda c, cg, nv: (cg[c], 0, 0, 0))
    clamp = lambda c, cg, nv: (jnp.minimum(c, nv[0] - 1), 0)
    y_sorted = pl.pallas_call(
        _group_experts_kernel,
        grid_spec=pltpu.PrefetchScalarGridSpec(
            num_scalar_prefetch=2,
            grid=(n_chunks,),
            in_specs=[pl.BlockSpec((ch, w), clamp), pl.BlockSpec(ng.shape, lambda c, cg, nv: (0, 0)),
                      wspec(d, f), wspec(d, f), wspec(f, d)],
            out_specs=pl.BlockSpec((ch, d), lambda c, cg, nv: (c, 0)),
        ),
        out_shape=jax.ShapeDtypeStruct((n_pad, d), F32),
        compiler_params=_params("arbitrary"),
        name="moe_experts",
    )(chunk_group, n_valid, sorted_rows, ng,
      wg.reshape(N_EXPERT_GROUPS, EXPERTS_PER_GROUP, d, f), wu.reshape(N_EXPERT_GROUPS, EXPERTS_PER_GROUP, d, f),
      wd.reshape(N_EXPERT_GROUPS, EXPERTS_PER_GROUP, f, d))

    return pl.pallas_call(
        _gather_rows_kernel,
        grid=(nt,),
        in_specs=[pos_spec, any_spec],
        out_specs=tile(d),
        out_shape=jax.ShapeDtypeStruct((n, d), F32),
        scratch_shapes=[pltpu.SemaphoreType.DMA],
        compiler_params=_params("arbitrary"),
        name="moe_gather",
    )(pos, y_sorted)


def _moe_kernel(x_ref, oa_ref, op_ref, wo_ref, ng_ref, rw_ref, rb_ref, wg_ref, wu_ref, wd_ref,
                y_ref, h2_s, comb_s, acc_s):
    e = pl.program_id(1)

    @pl.when(e == 0)
    def _():
        x1, h2, comb, _ = _merge(x_ref, oa_ref, op_ref, wo_ref, ng_ref, rw_ref, rb_ref)
        h2_s[...] = h2.astype(BF16)
        comb_s[...] = comb
        acc_s[...] = x1

    hb = h2_s[...]
    gate = _dot(hb, wg_ref[...])
    a = gate * (1.0 / (1.0 + jnp.exp(-gate))) * _dot(hb, wu_ref[...])
    lane = lax.broadcasted_iota(jnp.int32, comb_s.shape, 1)
    c = jnp.sum(jnp.where(lane == e + N_EXPERT_GROUPS, comb_s[...], 0.0), axis=-1, keepdims=True)
    acc_s[...] += _dot((a * c).astype(BF16), wd_ref[...])

    @pl.when(e == pl.num_programs(1) - 1)
    def _():
        y_ref[...] = acc_s[...]


def _params(*sem):
    return pltpu.CompilerParams(dimension_semantics=sem, vmem_limit_bytes=VMEM_LIMIT)


def _full(shape):
    return pl.BlockSpec(shape, lambda *_: (0,) * len(shape))


def _rope_tables(pos):
    inv_freq = ROPE_THETA ** (-jnp.arange(0, HEAD_DIM, 2, dtype=F32) / HEAD_DIM)
    ang = pos.astype(F32)[:, None] * inv_freq[None, :]
    cos, sin = jnp.cos(ang), jnp.sin(ang)
    reps = LANES // HEAD_DIM
    return jnp.tile(jnp.concatenate([cos, cos], 1), (1, reps)), jnp.tile(jnp.concatenate([-sin, sin], 1), (1, reps))


def _moe_call(x, oa, op, wo, ng, rw, rb, wg, wu, wd, rows):
    n, d = x.shape
    f = wg.shape[-1]
    return pl.pallas_call(
        _moe_kernel,
        grid=(n // rows, N_EXPERTS),
        in_specs=[
            pl.BlockSpec((rows, d), lambda i, e: (i, 0)),
            pl.BlockSpec((rows, D_ATTN), lambda i, e: (i, 0)),
            pl.BlockSpec((rows, D_POOL), lambda i, e: (i, 0)),
            _full(wo.shape), _full(ng.shape), _full(rw.shape), _full(rb.shape),
            pl.BlockSpec((None, d, f), lambda i, e: (e, 0, 0)),
            pl.BlockSpec((None, d, f), lambda i, e: (e, 0, 0)),
            pl.BlockSpec((None, f, d), lambda i, e: (e, 0, 0)),
        ],
        out_specs=pl.BlockSpec((rows, d), lambda i, e: (i, 0)),
        out_shape=jax.ShapeDtypeStruct((n, d), F32),
        scratch_shapes=[pltpu.VMEM((rows, d), BF16), pltpu.VMEM((rows, ROUTER_LANES), F32), pltpu.VMEM((rows, d), F32)],
        compiler_params=_params("parallel", "arbitrary"),
        name="merge_moe",
    )(x, oa, op, wo, ng, rw, rb, wg, wu, wd)


def kernel(x_prompt, x_sample, cache_k, cache_v, state_pool, page_table, meta_tokens, norm_attn_g, w_in, q_norm_g, k_norm_g, lambda_q1, lambda_k1, lambda_q2, lambda_k2, subln_g, pool_w, pool_scale, w_out, norm_ffn_g, router_group_w, router_group_b, router_expert_w, router_expert_b, expert_w_gate, expert_w_up, expert_w_down):
    B, SEQ, D = x_prompt.shape
    Bd, S_dec, _ = x_sample.shape
    n_phys, page = cache_k.shape[1], cache_k.shape[2]
    n_pages = page_table.shape[1]
    past_len = n_pages * page
    assert cache_k.shape[0] == 1 and N_META == MAX_WINDOW

    ng = norm_attn_g[0][None, :]
    win = w_in[0].astype(BF16)
    reps = D_QK // HEAD_DIM
    qg = jnp.tile(q_norm_g[0], reps)[None, :]
    kg = jnp.tile(k_norm_g[0], reps)[None, :]
    grp = jnp.arange(D_QK) // HEAD_DIM
    gmat = jnp.where(grp[:, None] == grp[None, :], 1.0 / HEAD_DIM, 0.0).astype(BF16)
    pw = pool_w[0].astype(BF16)
    ps = pool_scale[0][None, :]
    lq1, lk1, lq2, lk2 = (a[0][None, :] for a in (lambda_q1, lambda_k1, lambda_q2, lambda_k2))
    sg = subln_g[0][None, :]
    wo = w_out[0].astype(BF16)
    nfg = norm_ffn_g[0][None, :]
    pad = ROUTER_LANES - N_EXPERT_GROUPS - N_EXPERTS
    rw = jnp.concatenate([router_group_w[0], router_expert_w[0], jnp.zeros((D, pad), F32)], axis=1)
    rw_hi = rw.astype(BF16)
    rw = jnp.stack([rw_hi, (rw - rw_hi.astype(F32)).astype(BF16)])
    rb = jnp.concatenate([router_group_b[0], router_expert_b[0], jnp.zeros((pad,), F32)])[None, :]
    wg = expert_w_gate[0].astype(BF16)
    wu = expert_w_up[0].astype(BF16)
    wd = expert_w_down[0].astype(BF16)

    n_s = Bd * S_dec
    n_small = n_s + N_META
    xs_t = jnp.transpose(x_sample, (1, 0, 2)).reshape(n_s, D)
    x_small = jnp.concatenate([xs_t, meta_tokens.astype(F32)], axis=0)
    pos_small = jnp.concatenate([jnp.repeat(past_len + jnp.arange(S_dec), Bd), jnp.arange(N_META)])
    cos_s, sin_s = _rope_tables(pos_small)
    st_t = jnp.transpose(state_pool[0], (1, 0, 2))
    q_s, k_s, v_s, u_s, opool_s = pl.pallas_call(
        functools.partial(_proj_small_kernel, n_dec=Bd, s_dec=S_dec),
        out_shape=[jax.ShapeDtypeStruct((n_small, D_QK), F32)] * 4 + [jax.ShapeDtypeStruct((n_s, D_POOL), BF16)],
        compiler_params=pltpu.CompilerParams(vmem_limit_bytes=VMEM_LIMIT),
        name="proj_small",
    )(x_small, cos_s, sin_s, st_t, ng, win, gmat, qg, kg, pw, ps)
    k_meta, v_meta, u_meta = k_s[n_s:], v_s[n_s:], u_s[n_s:]

    tq = min(ATTN_ROWS, SEQ)
    tp = tq
    cos_p, sin_p = _rope_tables(N_META + jnp.arange(SEQ))
    row_spec = lambda w: pl.BlockSpec((None, tp, w), lambda b, i: (b, i, 0))
    tab_spec = pl.BlockSpec((tp, LANES), lambda b, i: (i, 0))
    vt_spec = pl.BlockSpec((None, N_HEADS, None, V_DIM, tp), lambda b, i: (b, 0, i, 0, 0))
    q_p, kb_p, vt_p, kf_p, vf_p, opool_p, utail_p = pl.pallas_call(
        _proj_prompt_kernel,
        grid=(B, SEQ // tp),
        in_specs=[row_spec(D), tab_spec, tab_spec, _full(u_meta.shape), _full(ng.shape), _full(win.shape),
                  _full(gmat.shape), _full(qg.shape), _full(kg.shape), _full(pw.shape), _full(ps.shape)],
        out_specs=[row_spec(D_QK), row_spec(D_QK), vt_spec, row_spec(D_QK), row_spec(D_ATTN), row_spec(D_POOL),
                   pl.BlockSpec((None, MAX_WINDOW, D_POOL), lambda b, i: (b, 0, 0))],
        out_shape=[jax.ShapeDtypeStruct((B, SEQ, D_QK), BF16)] * 2
        + [jax.ShapeDtypeStruct((B, N_HEADS, SEQ // tp, V_DIM, tp), BF16)]
        + [jax.ShapeDtypeStruct((B, SEQ, D_QK), F32)] * 2
        + [jax.ShapeDtypeStruct((B, SEQ, D_POOL), BF16), jax.ShapeDtypeStruct((B, MAX_WINDOW, D_POOL), F32)],
        scratch_shapes=[pltpu.VMEM((MAX_WINDOW + tp, D_POOL), F32)],
        compiler_params=_params("parallel", "arbitrary"),
        name="proj_prompt",
    )(x_prompt, cos_p, sin_p, u_meta, ng, win, gmat, qg, kg, pw, ps)

    km_pad = jnp.zeros((LANES, D_QK), BF16).at[:N_META].set(k_meta.astype(BF16))
    vm_pad = jnp.zeros((LANES, D_ATTN), BF16).at[:N_META].set(v_meta.astype(BF16))
    vmt_pad = jnp.transpose(vm_pad.reshape(LANES, N_HEADS, V_DIM), (1, 2, 0))
    vec_spec3 = lambda w: pl.BlockSpec((1, w), lambda b, h, i: (0, 0))
    oatt_p = pl.pallas_call(
        _attn_prompt_kernel,
        grid=(B, N_HEADS, SEQ // tq),
        in_specs=[
            pl.BlockSpec((None, tq, V_DIM), lambda b, h, i: (b, i, h)),
            pl.BlockSpec((None, SEQ, V_DIM), lambda b, h, i: (b, 0, h)),
            pl.BlockSpec((None, None, SEQ // tq, V_DIM, tq), lambda b, h, i: (b, h, 0, 0, 0)),
            pl.BlockSpec((LANES, V_DIM), lambda b, h, i: (0, h)),
            pl.BlockSpec((None, V_DIM, LANES), lambda b, h, i: (h, 0, 0)),
            vec_spec3(HEAD_DIM), vec_spec3(HEAD_DIM), vec_spec3(HEAD_DIM), vec_spec3(HEAD_DIM),
            pl.BlockSpec((V_DIM, 1), lambda b, h, i: (0, 0)),
        ],
        out_specs=pl.BlockSpec((None, tq, V_DIM), lambda b, h, i: (b, i, h)),
        out_shape=jax.ShapeDtypeStruct((B, SEQ, D_ATTN), BF16),
        scratch_shapes=[pltpu.VMEM((2 * tq, V_DIM), BF16), pltpu.VMEM((1, 2 * tq), F32), pltpu.VMEM((1, 2 * tq), F32),
                        pltpu.VMEM((V_DIM, 2 * tq), F32), pltpu.VMEM((tq, 2 * tq), F32), pltpu.VMEM((tq, 2 * tq), F32)],
        compiler_params=_params("parallel", "parallel", "arbitrary"),
        name="attn_prompt",
    )(q_p, kb_p, vt_p, km_pad, vmt_pad, lq1, lk1, lq2, lk2, sg.reshape(V_DIM, 1))

    hs = N_HEADS * S_dec
    qd = q_s[:n_s].reshape(S_dec, Bd, N_HEADS, 2, HEAD_DIM)
    qd = jnp.transpose(qd, (1, 3, 2, 0, 4))
    qmat = jnp.zeros((Bd, 2, hs, 2, HEAD_DIM), F32)
    qmat = qmat.at[:, 0, :, 0].set(qd[:, 0].reshape(Bd, hs, HEAD_DIM)).at[:, 1, :, 1].set(qd[:, 1].reshape(Bd, hs, HEAD_DIM))
    qmat = qmat.reshape(Bd, 2 * hs, V_DIM).astype(BF16)
    new_rows = lambda a: jnp.transpose(a[:n_s].reshape(S_dec, Bd, hs // S_dec * V_DIM), (1, 0, 2)).reshape(Bd, hs, V_DIM)
    kn_pad = jnp.zeros((Bd, LANES, V_DIM), BF16).at[:, :hs].set(new_rows(k_s).astype(BF16))
    vn_pad = jnp.zeros((Bd, LANES, V_DIM), BF16).at[:, :hs].set(new_rows(v_s).astype(BF16))
    ck = cache_k.reshape(n_phys, page * N_HEADS, V_DIM)
    cv = cache_v.reshape(n_phys, page * N_HEADS, V_DIM)
    n_pg = math.gcd(DECODE_PAGES, n_pages)
    vec_spec2 = lambda w: pl.BlockSpec((1, w), lambda b, g, pt: (0, 0))
    per_b = lambda r: pl.BlockSpec((None, r, V_DIM), lambda b, g, pt: (b, 0, 0))
    hbm_spec = pl.BlockSpec(memory_space=pl.ANY)
    page_buf = pltpu.VMEM((DECODE_SLOTS, n_pg, page * N_HEADS, V_DIM), F32)
    oatt_s = pl.pallas_call(
        functools.partial(_decode_kernel, n_pg=n_pg, s_dec=S_dec),
        grid_spec=pltpu.PrefetchScalarGridSpec(
            num_scalar_prefetch=1,
            grid=(Bd, n_pages // n_pg),
            in_specs=[per_b(2 * hs), per_b(LANES), per_b(LANES)] + [vec_spec2(HEAD_DIM)] * 4 + [vec_spec2(V_DIM)]
            + [hbm_spec, hbm_spec],
            out_specs=per_b(hs),
            scratch_shapes=[page_buf, page_buf, pltpu.SemaphoreType.DMA((DECODE_SLOTS, 2)),
                            pltpu.VMEM((2 * hs, 1), F32), pltpu.VMEM((2 * hs, 1), F32), pltpu.VMEM((2 * hs, V_DIM), F32)],
        ),
        out_shape=jax.ShapeDtypeStruct((Bd, hs, V_DIM), BF16),
        compiler_params=_params("arbitrary", "arbitrary"),
        name="attn_decode",
    )(page_table, qmat, kn_pad, vn_pad, lq1, lk1, lq2, lk2, sg, ck, cv)
    oatt_s = jnp.transpose(oatt_s.reshape(Bd, N_HEADS, S_dec, V_DIM), (2, 0, 1, 3)).reshape(n_s, D_ATTN)

    y_p = _routed_moe(x_prompt.reshape(B * SEQ, D), oatt_p.reshape(B * SEQ, D_ATTN), opool_p.reshape(B * SEQ, D_POOL),
                      wo, nfg, rw, rb, wg, wu, wd)
    y_s = _moe_call(xs_t, oatt_s, opool_s, wo, nfg, rw, rb, wg, wu, wd, min(MOE_ROWS, n_s))

    y_prompt = y_p.reshape(B, SEQ, D)
    y_sample = jnp.transpose(y_s.reshape(S_dec, Bd, D), (1, 0, 2))
    bc = lambda a: jnp.broadcast_to(a[None], (B,) + a.shape)
    new_k_prompt = jnp.concatenate([bc(k_meta), kf_p], axis=1).reshape(1, B, N_META + SEQ, N_HEADS, V_DIM)
    new_v_prompt = jnp.concatenate([bc(v_meta), vf_p], axis=1).reshape(1, B, N_META + SEQ, N_HEADS, V_DIM)
    new_pool_prompt = utail_p[:, MAX_WINDOW - POOL_STATE:][None]
    to_b = lambda a, w: jnp.transpose(a[:n_s].reshape(S_dec, Bd, w), (1, 0, 2))
    new_k_sample = to_b(k_s, D_QK).reshape(1, Bd, S_dec, N_HEADS, V_DIM)
    new_v_sample = to_b(v_s, D_ATTN).reshape(1, Bd, S_dec, N_HEADS, V_DIM)
    u_new = to_b(u_s, D_POOL)
    new_pool_sample = jnp.concatenate([state_pool[0].astype(F32), u_new], axis=1)[:, S_dec:][None]
    return (y_prompt, y_sample, new_k_prompt, new_v_prompt, new_pool_prompt, new_k_sample, new_v_sample,
            new_pool_sample)
```

```python
import functools
import math

import jax
import jax.numpy as jnp
from jax import lax
from jax.experimental import pallas as pl
from jax.experimental.pallas import tpu as pltpu

N_META = 16
N_HEADS = 4
HEAD_DIM = 64
V_DIM = 2 * HEAD_DIM
D_QK = N_HEADS * 2 * HEAD_DIM
D_ATTN = N_HEADS * V_DIM
POOL_WINDOWS = (2, 4, 8, 16)
POOL_GROUP = 128
MAX_WINDOW = max(POOL_WINDOWS)
POOL_STATE = MAX_WINDOW - 1
D_POOL = POOL_GROUP * len(POOL_WINDOWS)
N_EXPERT_GROUPS = 4
EXPERTS_PER_GROUP = 4
N_EXPERTS = N_EXPERT_GROUPS * EXPERTS_PER_GROUP
ROPE_THETA = 10000.0
EPS = 1e-6
NEG = -1e30
LOG2E = 1.4426950408889634
LAM_INIT = 0.8 - 0.6 * math.exp(-0.3 * 0)
Q_SCALE = HEAD_DIM ** -0.5 * LOG2E

LANES = 128
ROUTER_LANES = LANES
VMEM_LIMIT = 48 * 1024 * 1024

ATTN_ROWS = 512
ATTN_LANE_BLOCK = 256
MOE_ROWS = 512
MOE_CHUNK = 512
ROW_DMA_UNROLL = 8
DECODE_PAGES = 8
DECODE_SLOTS = 3

F32 = jnp.float32
BF16 = jnp.bfloat16


def _dot(a, b):
    return jnp.dot(a, b, preferred_element_type=F32)


def _dot_nt(a, b):
    return lax.dot_general(a, b, (((1,), (1,)), ((), ())), preferred_element_type=F32)


def _rms(x, g):
    return x * lax.rsqrt(jnp.mean(x * x, axis=-1, keepdims=True) + EPS) * g


def _qk_norm_rope(z, gmat, g_t, cos_t, sin_t):
    ms = _dot((z * z).astype(BF16), gmat)
    xn = z * lax.rsqrt(ms + EPS) * g_t
    outs = []
    for j in range(z.shape[1] // LANES):
        xb = xn[:, j * LANES:(j + 1) * LANES]
        lane = lax.broadcasted_iota(jnp.int32, xb.shape, 1)
        first_half = (lane % HEAD_DIM) < (HEAD_DIM // 2)
        rot = jnp.where(first_half, pltpu.roll(xb, LANES - HEAD_DIM // 2, 1), pltpu.roll(xb, HEAD_DIM // 2, 1))
        outs.append(xb * cos_t + rot * sin_t)
    return jnp.concatenate(outs, axis=1)


def _project(x, ng, win_ref, gmat, qg, kg, cos_t, sin_t):
    h = _rms(x, ng).astype(BF16)
    q = _qk_norm_rope(_dot(h, win_ref[:, 0:D_QK]), gmat, qg, cos_t, sin_t) * Q_SCALE
    k = _qk_norm_rope(_dot(h, win_ref[:, D_QK:2 * D_QK]), gmat, kg, cos_t, sin_t)
    v = _dot(h, win_ref[:, 2 * D_QK:2 * D_QK + D_ATTN])
    u = _dot(h, win_ref[:, 2 * D_QK + D_ATTN:])
    return q, k, v, u


def _proj_small_kernel(x_ref, cos_ref, sin_ref, st_ref, ng_ref, win_ref, gmat_ref, qg_ref, kg_ref, pw_ref, ps_ref,
                       q_ref, k_ref, v_ref, u_ref, op_ref, *, n_dec, s_dec):
    q, k, v, u = _project(x_ref[...], ng_ref[...], win_ref, gmat_ref[...], qg_ref[...], kg_ref[...],
                          cos_ref[...], sin_ref[...])
    q_ref[...] = q
    k_ref[...] = k
    v_ref[...] = v
    u_ref[...] = u
    for g, w in enumerate(POOL_WINDOWS):
        cs = slice(g * POOL_GROUP, (g + 1) * POOL_GROUP)
        us = [u[s * n_dec:(s + 1) * n_dec, cs] for s in range(s_dec)]
        need = {w - 1 - s for s in range(s_dec) if w - 1 - s > 0}
        suffix = {0: jnp.zeros_like(us[0])}
        run = jnp.zeros_like(us[0])
        for c in range(1, max(need) + 1 if need else 1):
            run = run + st_ref[POOL_STATE - c, :, cs]
            if c in need:
                suffix[c] = run
        for s in range(s_dec):
            win = suffix[max(w - 1 - s, 0)]
            for sp in range(max(0, s - w + 1), s + 1):
                win = win + us[sp]
            d = win * (1.0 / w) - us[s]
            o = _dot(d.astype(BF16), pw_ref[g]) * ps_ref[:, cs]
            op_ref[s * n_dec:(s + 1) * n_dec, cs] = o.astype(BF16)


def _proj_prompt_kernel(x_ref, cos_ref, sin_ref, um_ref, ng_ref, win_ref, gmat_ref, qg_ref, kg_ref, pw_ref, ps_ref,
                        q_ref, kb_ref, vt_ref, kf_ref, vf_ref, op_ref, ut_ref, ubuf):
    i = pl.program_id(1)
    t = x_ref.shape[0]
    q, k, v, u = _project(x_ref[...], ng_ref[...], win_ref, gmat_ref[...], qg_ref[...], kg_ref[...],
                          cos_ref[...], sin_ref[...])
    q_ref[...] = q.astype(BF16)
    kb_ref[...] = k.astype(BF16)
    for h in range(N_HEADS):
        kf_ref[:, h, :] = k[:, h * V_DIM:(h + 1) * V_DIM]
        vf_ref[:, h, :] = v[:, h * V_DIM:(h + 1) * V_DIM]
    for h in range(N_HEADS):
        vt_ref[h] = v[:, h * V_DIM:(h + 1) * V_DIM].T.astype(BF16)

    @pl.when(i == 0)
    def _():
        ubuf[0:MAX_WINDOW, :] = um_ref[...]

    ubuf[MAX_WINDOW:MAX_WINDOW + t, :] = u
    for g, w in enumerate(POOL_WINDOWS):
        cs = slice(g * POOL_GROUP, (g + 1) * POOL_GROUP)
        e = ubuf[:, cs]
        a = e + pltpu.roll(e, 1, 0)
        sh = 2
        while sh < w:
            a = a + pltpu.roll(a, sh, 0)
            sh *= 2
        d = a[MAX_WINDOW:, :] * (1.0 / w) - e[MAX_WINDOW:, :]
        o = _dot(d.astype(BF16), pw_ref[g]) * ps_ref[:, cs]
        op_ref[:, cs] = o.astype(BF16)
    tail = ubuf[t:t + MAX_WINDOW, :]
    ut_ref[...] = tail
    ubuf[0:MAX_WINDOW, :] = tail


def _diff_lambda(lq1_ref, lk1_ref, lq2_ref, lk2_ref):
    a = jnp.sum(lq1_ref[...] * lk1_ref[...], axis=-1, keepdims=True)
    b = jnp.sum(lq2_ref[...] * lk2_ref[...], axis=-1, keepdims=True)
    return jnp.exp(a) - jnp.exp(b) + LAM_INIT


def _online_update(s, v, m_s, l_s, acc_s):
    m_old = m_s[...]
    m_new = jnp.maximum(m_old, jnp.max(s, axis=-1, keepdims=True))
    alpha = jnp.exp2(m_old - m_new)
    p = jnp.exp2(s - m_new)
    l_s[...] = alpha * l_s[...] + jnp.sum(p, axis=-1, keepdims=True)
    acc_s[...] = alpha * acc_s[...] + _dot(p.astype(BF16), v)
    m_s[...] = m_new


def _diff_finalize(n, lam, sg, l_s, acc_s):
    o = acc_s[0:n, :] / l_s[0:n, :] - lam * (acc_s[n:2 * n, :] / l_s[n:2 * n, :])
    return _rms(o, sg) * (1.0 - LAM_INIT)


def _attn_prompt_kernel(q_ref, k_ref, vt_ref, km_ref, vmt_ref, lq1_ref, lk1_ref, lq2_ref, lk2_ref, sgc_ref,
                        o_ref, qs, m_s, l_s, acc_s, sa, sb):
    qi = pl.program_id(2)
    tq = q_ref.shape[0]
    q = q_ref[...]
    lane = lax.broadcasted_iota(jnp.int32, q.shape, 1)
    zero = jnp.zeros_like(q)
    qs[0:tq, :] = jnp.where(lane < HEAD_DIM, q, zero)
    qs[tq:2 * tq, :] = jnp.where(lane >= HEAD_DIM, q, zero)
    n_blk = 2 * tq // ATTN_LANE_BLOCK

    blocks = [slice(j * ATTN_LANE_BLOCK, (j + 1) * ATTN_LANE_BLOCK) for j in range(n_blk)]
    m_s[...] = jnp.full(m_s.shape, NEG, F32)
    l_s[...] = jnp.zeros(l_s.shape, F32)
    acc_s[...] = jnp.zeros(acc_s.shape, F32)

    def scores_to(kk, buf):
        k = k_ref[pl.ds(pl.multiple_of(kk * tq, tq), tq), :]
        for cs in blocks:
            buf[:, cs] = _dot_nt(k, qs[cs, :])

    def update(cs, parts):
        m_old = m_s[:, cs]
        m_new = m_old
        for s, _ in parts:
            m_new = jnp.maximum(m_new, jnp.max(s, axis=0, keepdims=True))
        alpha = jnp.exp2(m_old - m_new)
        l = alpha * l_s[:, cs]
        acc = alpha * acc_s[:, cs]
        for s, vt in parts:
            p = jnp.exp2(s - m_new)
            l = l + jnp.sum(p, axis=0, keepdims=True)
            acc = acc + _dot(vt, p.astype(BF16))
        l_s[:, cs] = l
        acc_s[:, cs] = acc
        m_s[:, cs] = m_new

    def process(buf, kk):
        vt = vt_ref[kk]
        for cs in blocks:
            update(cs, [(buf[:, cs], vt)])

    def last_step(buf):
        vt = vt_ref[qi]
        for j, cs in enumerate(blocks):
            s_meta = _dot_nt(km_ref[...], qs[cs, :])
            mkey = lax.broadcasted_iota(jnp.int32, s_meta.shape, 0)
            s_meta = jnp.where(mkey < N_META, s_meta, NEG)
            s = buf[:, cs]
            key = lax.broadcasted_iota(jnp.int32, s.shape, 0)
            ql = lax.broadcasted_iota(jnp.int32, s.shape, 1) + j * ATTN_LANE_BLOCK
            tok = jnp.where(ql >= tq, ql - tq, ql)
            s = jnp.where(key <= tok, s, NEG)
            update(cs, [(s_meta, vmt_ref[...]), (s, vt)])

    scores_to(0, sa)

    def pair(pp, carry):
        k0 = 2 * pp
        scores_to(k0 + 1, sb)
        process(sa, k0)
        scores_to(k0 + 2, sa)
        process(sb, k0 + 1)
        return carry

    lax.fori_loop(0, qi // 2, pair, 0)

    @pl.when(qi % 2 == 1)
    def _():
        scores_to(qi, sb)
        process(sa, qi - 1)
        last_step(sb)

    @pl.when(qi % 2 == 0)
    def _():
        last_step(sa)

    lam = _diff_lambda(lq1_ref, lk1_ref, lq2_ref, lk2_ref)
    o = acc_s[:, 0:tq] / l_s[:, 0:tq] - lam * (acc_s[:, tq:2 * tq] / l_s[:, tq:2 * tq])
    on = o * lax.rsqrt(jnp.mean(o * o, axis=0, keepdims=True) + EPS) * sgc_ref[...] * (1.0 - LAM_INIT)
    o_ref[...] = on.T.astype(BF16)


def _decode_kernel(pt_ref, q_ref, kn_ref, vn_ref, lq1_ref, lk1_ref, lq2_ref, lk2_ref, sg_ref, ck_ref, cv_ref,
                   o_ref, kbuf, vbuf, sems, m_s, l_s, acc_s, *, n_pg, s_dec):
    b = pl.program_id(0)
    g = pl.program_id(1)
    n_g = pl.num_programs(1)
    step = b * n_g + g
    n_steps = pl.num_programs(0) * n_g
    n_rows = q_ref.shape[0]
    half = n_rows // 2

    def page_copies(st, slot):
        bb = st // n_g
        g0 = (st % n_g) * n_pg
        copies = []
        for j in range(n_pg):
            pid = pt_ref[bb, g0 + j]
            copies.append(pltpu.make_async_copy(ck_ref.at[pid], kbuf.at[slot, j], sems.at[slot, 0]))
            copies.append(pltpu.make_async_copy(cv_ref.at[pid], vbuf.at[slot, j], sems.at[slot, 1]))
        return copies

    def start_step(st):
        for cp in page_copies(st, st % DECODE_SLOTS):
            cp.start()

    @pl.when(step == 0)
    def _():
        for d in range(DECODE_SLOTS - 1):
            @pl.when(d < n_steps)
            def _():
                start_step(d)

    @pl.when(step + DECODE_SLOTS - 1 < n_steps)
    def _():
        start_step(step + DECODE_SLOTS - 1)

    @pl.when(g == 0)
    def _():
        m_s[...] = jnp.full(m_s.shape, NEG, F32)
        l_s[...] = jnp.zeros(l_s.shape, F32)
        acc_s[...] = jnp.zeros(acc_s.shape, F32)

    slot = step % DECODE_SLOTS
    for cp in page_copies(step, slot):
        cp.wait()
    k_refs = [kbuf.at[slot, j] for j in range(n_pg)]
    v_refs = [vbuf.at[slot, j] for j in range(n_pg)]

    q = q_ref[...]
    page_cols = kbuf.shape[2]
    row = lax.broadcasted_iota(jnp.int32, (n_rows, page_cols), 0)
    col = lax.broadcasted_iota(jnp.int32, (n_rows, page_cols), 1)
    head_ok = (col % N_HEADS) == ((row % half) // s_dec)
    ss = []
    for j in range(n_pg):
        s = _dot_nt(q, k_refs[j][...].astype(BF16))
        ss.append(jnp.where(head_ok, s, NEG))
    m_old = m_s[...]
    m_new = m_old
    for s in ss:
        m_new = jnp.maximum(m_new, jnp.max(s, axis=-1, keepdims=True))
    alpha = jnp.exp2(m_old - m_new)
    l = alpha * l_s[...]
    acc = alpha * acc_s[...]
    for j in range(n_pg):
        p = jnp.exp2(ss[j] - m_new)
        l = l + jnp.sum(p, axis=-1, keepdims=True)
        acc = acc + _dot(p.astype(BF16), v_refs[j][...].astype(BF16))
    m_s[...] = m_new
    l_s[...] = l
    acc_s[...] = acc

    @pl.when(g == pl.num_programs(1) - 1)
    def _():
        s = _dot_nt(q, kn_ref[...])
        r2 = lax.broadcasted_iota(jnp.int32, s.shape, 0)
        c2 = lax.broadcasted_iota(jnp.int32, s.shape, 1)
        ok = (c2 < s_dec * N_HEADS) & ((c2 % N_HEADS) == ((r2 % half) // s_dec)) & ((c2 // N_HEADS) <= (r2 % s_dec))
        _online_update(jnp.where(ok, s, NEG), vn_ref[...], m_s, l_s, acc_s)
        lam = _diff_lambda(lq1_ref, lk1_ref, lq2_ref, lk2_ref)
        o_ref[...] = _diff_finalize(half, lam, sg_ref[...], l_s, acc_s).astype(BF16)


def _route(logits):
    ninf = float("-inf")
    lane = lax.broadcasted_iota(jnp.int32, logits.shape, 1)
    lane_f = lane.astype(F32)
    big = float(ROUTER_LANES)
    is_g = lane < N_EXPERT_GROUPS
    gl = jnp.where(is_g, logits, ninf)
    gmax = jnp.max(gl, axis=-1, keepdims=True)
    gidx = jnp.min(jnp.where(gl == gmax, lane_f, big), axis=-1, keepdims=True)
    gsum = jnp.sum(jnp.where(is_g, jnp.exp(logits - gmax), 0.0), axis=-1, keepdims=True)
    gprob = 1.0 / gsum
    egrp = ((lane - N_EXPERT_GROUPS) // EXPERTS_PER_GROUP).astype(F32)
    is_e = (lane >= N_EXPERT_GROUPS) & (lane < N_EXPERT_GROUPS + N_EXPERTS) & (egrp == gidx)
    ev = jnp.where(is_e, logits, ninf)
    t1 = jnp.max(ev, axis=-1, keepdims=True)
    i1 = jnp.min(jnp.where(ev == t1, lane_f, big), axis=-1, keepdims=True)
    ev2 = jnp.where(lane_f == i1, ninf, ev)
    t2 = jnp.max(ev2, axis=-1, keepdims=True)
    i2 = jnp.min(jnp.where(ev2 == t2, lane_f, big), axis=-1, keepdims=True)
    r = jnp.exp(t2 - t1)
    w1 = gprob / (1.0 + r)
    w2 = w1 * r
    return jnp.where(lane_f == i1, w1, 0.0) + jnp.where(lane_f == i2, w2, 0.0), gidx


def _merge(x_ref, oa_ref, op_ref, wo_ref, ng_ref, rw_ref, rb_ref):
    x1 = x_ref[...] + _dot(oa_ref[...], wo_ref[0:D_ATTN, :]) + _dot(op_ref[...], wo_ref[D_ATTN:, :])
    h2 = _rms(x1, ng_ref[...])
    h_hi = h2.astype(BF16)
    h_lo = (h2 - h_hi.astype(F32)).astype(BF16)
    logits = _dot(h_hi, rw_ref[0]) + _dot(h_lo, rw_ref[0]) + _dot(h_hi, rw_ref[1]) + rb_ref[...]
    comb, gidx = _route(logits)
    return x1, h2, comb, gidx


def _router_kernel(x_ref, oa_ref, op_ref, wo_ref, ng_ref, rw_ref, rb_ref, p_ref, meta_ref, cnt_ref, carry):
    i = pl.program_id(0)
    t, d = x_ref.shape

    @pl.when(i == 0)
    def _():
        carry[...] = jnp.zeros(carry.shape, F32)

    x1, _, comb, gidx = _merge(x_ref, oa_ref, op_ref, wo_ref, ng_ref, rw_ref, rb_ref)
    lane = lax.broadcasted_iota(jnp.int32, comb.shape, 1)
    onehot = jnp.where(lane.astype(F32) == gidx, 1.0, 0.0)
    r = lax.broadcasted_iota(jnp.int32, (t, t), 0)
    c = lax.broadcasted_iota(jnp.int32, (t, t), 1)
    before = jnp.where(c < r, 1.0, 0.0).astype(BF16)
    prefix = _dot(before, onehot.astype(BF16))
    rank = jnp.sum(onehot * (prefix + carry[...]), axis=-1, keepdims=True)
    carry[...] = carry[...] + jnp.sum(onehot, axis=0, keepdims=True)
    routing = comb + jnp.where(lane == 0, gidx, 0.0) + jnp.where(lane == 1, rank, 0.0)
    p_ref[:, 0:d] = x1
    p_ref[:, d:] = routing
    meta_ref[...] = routing.T[0:8, :]
    cnt_ref[...] = carry[...]


def _row_copy(src_ref, src_row, dst_ref, dst_row, sem):
    return pltpu.make_async_copy(src_ref.at[pl.ds(src_row, 1)], dst_ref.at[pl.ds(dst_row, 1)], sem)


def _scatter_rows_kernel(pos_ref, p_ref, init_ref, out_ref, sem):
    del init_ref
    t = p_ref.shape[0]

    def start(r, carry):
        _row_copy(p_ref, r, out_ref, pos_ref[0, r], sem).start()
        return carry

    def wait(r, carry):
        _row_copy(p_ref, r, out_ref, pos_ref[0, r], sem).wait()
        return carry

    lax.fori_loop(0, t, start, 0, unroll=ROW_DMA_UNROLL)
    lax.fori_loop(0, t, wait, 0, unroll=ROW_DMA_UNROLL)


def _gather_rows_kernel(pos_ref, src_ref, y_ref, sem):
    t = y_ref.shape[0]

    def start(r, carry):
        _row_copy(src_ref, pos_ref[0, r], y_ref, r, sem).start()
        return carry

    def wait(r, carry):
        _row_copy(src_ref, pos_ref[0, r], y_ref, r, sem).wait()
        return carry

    lax.fori_loop(0, t, start, 0, unroll=ROW_DMA_UNROLL)
    lax.fori_loop(0, t, wait, 0, unroll=ROW_DMA_UNROLL)


def _group_experts_kernel(cg_ref, nv_ref, p_ref, ng_ref, wg_ref, wu_ref, wd_ref, y_ref):
    c = pl.program_id(0)
    d = y_ref.shape[1]

    @pl.when(c < nv_ref[0])
    def _():
        g = cg_ref[c]
        x1 = p_ref[:, 0:d]
        routing = p_ref[:, d:]
        hb = _rms(x1, ng_ref[...]).astype(BF16)
        lane = lax.broadcasted_iota(jnp.int32, routing.shape, 1)
        y = x1
        for j in range(EXPERTS_PER_GROUP):
            gate = _dot(hb, wg_ref[j])
            a = gate * (1.0 / (1.0 + jnp.exp(-gate))) * _dot(hb, wu_ref[j])
            sel = lane == N_EXPERT_GROUPS + g * EXPERTS_PER_GROUP + j
            cj = jnp.sum(jnp.where(sel, routing, 0.0), axis=-1, keepdims=True)
            y = y + _dot((a * cj).astype(BF16), wd_ref[j])
        y_ref[...] = y

    @pl.when(c >= nv_ref[0])
    def _():
        y_ref[...] = jnp.zeros(y_ref.shape, F32)


def _routed_moe(x, oa, op, wo, ng, rw, rb, wg, wu, wd):
    n, d = x.shape
    f = wg.shape[-1]
    t, ch = MOE_ROWS, MOE_CHUNK
    w = d + ROUTER_LANES
    nt = n // t
    tile = lambda width: pl.BlockSpec((t, width), lambda i: (i, 0))
    packed, meta, counts = pl.pallas_call(
        _router_kernel,
        grid=(nt,),
        in_specs=[tile(d), tile(D_ATTN), tile(D_POOL), _full(wo.shape), _full(ng.shape), _full(rw.shape),
                  _full(rb.shape)],
        out_specs=[tile(w), pl.BlockSpec((None, 8, t), lambda i: (i, 0, 0)), _full((1, ROUTER_LANES))],
        out_shape=[jax.ShapeDtypeStruct((n, w), F32), jax.ShapeDtypeStruct((nt, 8, t), F32),
                   jax.ShapeDtypeStruct((1, ROUTER_LANES), F32)],
        scratch_shapes=[pltpu.VMEM((1, ROUTER_LANES), F32)],
        compiler_params=_params("arbitrary"),
        name="moe_router",
    )(x, oa, op, wo, ng, rw, rb)

    gid = meta[:, 0, :].astype(jnp.int32)
    rank = meta[:, 1, :].astype(jnp.int32)
    cnt = counts[0, :N_EXPERT_GROUPS].astype(jnp.int32)
    cap = (cnt + ch - 1) // ch * ch
    ends = jnp.cumsum(cap)
    base = ends - cap
    pos = rank
    for g in range(N_EXPERT_GROUPS):
        pos = pos + jnp.where(gid == g, base[g], 0)
    pos = pos.reshape(nt, 1, t)
    n_chunks = n // ch + N_EXPERT_GROUPS
    n_pad = n_chunks * ch
    chunk_group = jnp.minimum(jnp.sum(jnp.arange(n_chunks)[:, None] >= (ends // ch)[None, :], axis=1),
                              N_EXPERT_GROUPS - 1).astype(jnp.int32)
    n_valid = (ends[-1:] // ch).astype(jnp.int32)

    pos_spec = pl.BlockSpec((None, 1, t), lambda i: (i, 0, 0), memory_space=pltpu.SMEM)
    any_spec = pl.BlockSpec(memory_space=pl.ANY)
    sorted_rows = pl.pallas_call(
        _scatter_rows_kernel,
        grid=(nt,),
        in_specs=[pos_spec, tile(w), any_spec],
        out_specs=any_spec,
        out_shape=jax.ShapeDtypeStruct((n_pad, w), F32),
        scratch_shapes=[pltpu.SemaphoreType.DMA],
        input_output_aliases={2: 0},
        compiler_params=_params("arbitrary"),
        name="moe_scatter",
    )(pos, packed, jnp.zeros((n_pad, w), F32))

    wspec = lambda a, b_: pl.BlockSpec((None, EXPERTS_PER_GROUP, a, b_), lambda c, cg, nv: (cg[c], 0, 0, 0))
    clamp = lambda c, cg, nv: (jnp.minimum(c, nv[0] - 1), 0)
    y_sorted = pl.pallas_call(
        _group_experts_kernel,
        grid_spec=pltpu.PrefetchScalarGridSpec(
            num_scalar_prefetch=2,
            grid=(n_chunks,),
            in_specs=[pl.BlockSpec((ch, w), clamp), pl.BlockSpec(ng.shape, lambda c, cg, nv: (0, 0)),
                      wspec(d, f), wspec(d, f), wspec(f, d)],
            out_specs=pl.BlockSpec((ch, d), lambda c, cg, nv: (c, 0)),
        ),
        out_shape=jax.ShapeDtypeStruct((n_pad, d), F32),
        compiler_params=_params("arbitrary"),
        name="moe_experts",
    )(chunk_group, n_valid, sorted_rows, ng,
      wg.reshape(N_EXPERT_GROUPS, EXPERTS_PER_GROUP, d, f), wu.reshape(N_EXPERT_GROUPS, EXPERTS_PER_GROUP, d, f),
      wd.reshape(N_EXPERT_GROUPS, EXPERTS_PER_GROUP, f, d))

    return pl.pallas_call(
        _gather_rows_kernel,
        grid=(nt,),
        in_specs=[pos_spec, any_spec],
        out_specs=tile(d),
        out_shape=jax.ShapeDtypeStruct((n, d), F32),
        scratch_shapes=[pltpu.SemaphoreType.DMA],
        compiler_params=_params("arbitrary"),
        name="moe_gather",
    )(pos, y_sorted)


def _moe_kernel(x_ref, oa_ref, op_ref, wo_ref, ng_ref, rw_ref, rb_ref, wg_ref, wu_ref, wd_ref,
                y_ref, h2_s, comb_s, acc_s):
    e = pl.program_id(1)

    @pl.when(e == 0)
    def _():
        x1, h2, comb, _ = _merge(x_ref, oa_ref, op_ref, wo_ref, ng_ref, rw_ref, rb_ref)
        h2_s[...] = h2.astype(BF16)
        comb_s[...] = comb
        acc_s[...] = x1

    hb = h2_s[...]
    gate = _dot(hb, wg_ref[...])
    a = gate * (1.0 / (1.0 + jnp.exp(-gate))) * _dot(hb, wu_ref[...])
    lane = lax.broadcasted_iota(jnp.int32, comb_s.shape, 1)
    c = jnp.sum(jnp.where(lane == e + N_EXPERT_GROUPS, comb_s[...], 0.0), axis=-1, keepdims=True)
    acc_s[...] += _dot((a * c).astype(BF16), wd_ref[...])

    @pl.when(e == pl.num_programs(1) - 1)
    def _():
        y_ref[...] = acc_s[...]


def _params(*sem):
    return pltpu.CompilerParams(dimension_semantics=sem, vmem_limit_bytes=VMEM_LIMIT)


def _full(shape):
    return pl.BlockSpec(shape, lambda *_: (0,) * len(shape))


def _rope_tables(pos):
    inv_freq = ROPE_THETA ** (-jnp.arange(0, HEAD_DIM, 2, dtype=F32) / HEAD_DIM)
    ang = pos.astype(F32)[:, None] * inv_freq[None, :]
    cos, sin = jnp.cos(ang), jnp.sin(ang)
    reps = LANES // HEAD_DIM
    return jnp.tile(jnp.concatenate([cos, cos], 1), (1, reps)), jnp.tile(jnp.concatenate([-sin, sin], 1), (1, reps))


def _moe_call(x, oa, op, wo, ng, rw, rb, wg, wu, wd, rows):
    n, d = x.shape
    f = wg.shape[-1]
    return pl.pallas_call(
        _moe_kernel,
        grid=(n // rows, N_EXPERTS),
        in_specs=[
            pl.BlockSpec((rows, d), lambda i, e: (i, 0)),
            pl.BlockSpec((rows, D_ATTN), lambda i, e: (i, 0)),
            pl.BlockSpec((rows, D_POOL), lambda i, e: (i, 0)),
            _full(wo.shape), _full(ng.shape), _full(rw.shape), _full(rb.shape),
            pl.BlockSpec((None, d, f), lambda i, e: (e, 0, 0)),
            pl.BlockSpec((None, d, f), lambda i, e: (e, 0, 0)),
            pl.BlockSpec((None, f, d), lambda i, e: (e, 0, 0)),
        ],
        out_specs=pl.BlockSpec((rows, d), lambda i, e: (i, 0)),
        out_shape=jax.ShapeDtypeStruct((n, d), F32),
        scratch_shapes=[pltpu.VMEM((rows, d), BF16), pltpu.VMEM((rows, ROUTER_LANES), F32), pltpu.VMEM((rows, d), F32)],
        compiler_params=_params("parallel", "arbitrary"),
        name="merge_moe",
    )(x, oa, op, wo, ng, rw, rb, wg, wu, wd)


def kernel(x_prompt, x_sample, cache_k, cache_v, state_pool, page_table, meta_tokens, norm_attn_g, w_in, q_norm_g, k_norm_g, lambda_q1, lambda_k1, lambda_q2, lambda_k2, subln_g, pool_w, pool_scale, w_out, norm_ffn_g, router_group_w, router_group_b, router_expert_w, router_expert_b, expert_w_gate, expert_w_up, expert_w_down):
    B, SEQ, D = x_prompt.shape
    Bd, S_dec, _ = x_sample.shape
    n_phys, page = cache_k.shape[1], cache_k.shape[2]
    n_pages = page_table.shape[1]
    past_len = n_pages * page
    assert cache_k.shape[0] == 1 and N_META == MAX_WINDOW

    ng = norm_attn_g[0][None, :]
    win = w_in[0].astype(BF16)
    reps = D_QK // HEAD_DIM
    qg = jnp.tile(q_norm_g[0], reps)[None, :]
    kg = jnp.tile(k_norm_g[0], reps)[None, :]
    grp = jnp.arange(D_QK) // HEAD_DIM
    gmat = jnp.where(grp[:, None] == grp[None, :], 1.0 / HEAD_DIM, 0.0).astype(BF16)
    pw = pool_w[0].astype(BF16)
    ps = pool_scale[0][None, :]
    lq1, lk1, lq2, lk2 = (a[0][None, :] for a in (lambda_q1, lambda_k1, lambda_q2, lambda_k2))
    sg = subln_g[0][None, :]
    wo = w_out[0].astype(BF16)
    nfg = norm_ffn_g[0][None, :]
    pad = ROUTER_LANES - N_EXPERT_GROUPS - N_EXPERTS
    rw = jnp.concatenate([router_group_w[0], router_expert_w[0], jnp.zeros((D, pad), F32)], axis=1)
    rw_hi = rw.astype(BF16)
    rw = jnp.stack([rw_hi, (rw - rw_hi.astype(F32)).astype(BF16)])
    rb = jnp.concatenate([router_group_b[0], router_expert_b[0], jnp.zeros((pad,), F32)])[None, :]
    wg = expert_w_gate[0].astype(BF16)
    wu = expert_w_up[0].astype(BF16)
    wd = expert_w_down[0].astype(BF16)

    n_s = Bd * S_dec
    n_small = n_s + N_META
    xs_t = jnp.transpose(x_sample, (1, 0, 2)).reshape(n_s, D)
    x_small = jnp.concatenate([xs_t, meta_tokens.astype(F32)], axis=0)
    pos_small = jnp.concatenate([jnp.repeat(past_len + jnp.arange(S_dec), Bd), jnp.arange(N_META)])
    cos_s, sin_s = _rope_tables(pos_small)
    st_t = jnp.transpose(state_pool[0], (1, 0, 2))
    q_s, k_s, v_s, u_s, opool_s = pl.pallas_call(
        functools.partial(_proj_small_kernel, n_dec=Bd, s_dec=S_dec),
        out_shape=[jax.ShapeDtypeStruct((n_small, D_QK), F32)] * 4 + [jax.ShapeDtypeStruct((n_s, D_POOL), BF16)],
        compiler_params=pltpu.CompilerParams(vmem_limit_bytes=VMEM_LIMIT),
        name="proj_small",
    )(x_small, cos_s, sin_s, st_t, ng, win, gmat, qg, kg, pw, ps)
    k_meta, v_meta, u_meta = k_s[n_s:], v_s[n_s:], u_s[n_s:]

    tq = min(ATTN_ROWS, SEQ)
    tp = tq
    cos_p, sin_p = _rope_tables(N_META + jnp.arange(SEQ))
    row_spec = lambda w: pl.BlockSpec((None, tp, w), lambda b, i: (b, i, 0))
    tab_spec = pl.BlockSpec((tp, LANES), lambda b, i: (i, 0))
    vt_spec = pl.BlockSpec((None, N_HEADS, None, V_DIM, tp), lambda b, i: (b, 0, i, 0, 0))
    head_spec = pl.BlockSpec((None, tp, N_HEADS, V_DIM), lambda b, i: (b, i, 0, 0))
    q_p, kb_p, vt_p, kf_p, vf_p, opool_p, utail_p = pl.pallas_call(
        _proj_prompt_kernel,
        grid=(B, SEQ // tp),
        in_specs=[row_spec(D), tab_spec, tab_spec, _full(u_meta.shape), _full(ng.shape), _full(win.shape),
                  _full(gmat.shape), _full(qg.shape), _full(kg.shape), _full(pw.shape), _full(ps.shape)],
        out_specs=[row_spec(D_QK), row_spec(D_QK), vt_spec, head_spec, head_spec, row_spec(D_POOL),
                   pl.BlockSpec((None, MAX_WINDOW, D_POOL), lambda b, i: (b, 0, 0))],
        out_shape=[jax.ShapeDtypeStruct((B, SEQ, D_QK), BF16)] * 2
        + [jax.ShapeDtypeStruct((B, N_HEADS, SEQ // tp, V_DIM, tp), BF16)]
        + [jax.ShapeDtypeStruct((B, SEQ, N_HEADS, V_DIM), F32)] * 2
        + [jax.ShapeDtypeStruct((B, SEQ, D_POOL), BF16), jax.ShapeDtypeStruct((B, MAX_WINDOW, D_POOL), F32)],
        scratch_shapes=[pltpu.VMEM((MAX_WINDOW + tp, D_POOL), F32)],
        compiler_params=_params("parallel", "arbitrary"),
        name="proj_prompt",
    )(x_prompt, cos_p, sin_p, u_meta, ng, win, gmat, qg, kg, pw, ps)

    km_pad = jnp.zeros((LANES, D_QK), BF16).at[:N_META].set(k_meta.astype(BF16))
    vm_pad = jnp.zeros((LANES, D_ATTN), BF16).at[:N_META].set(v_meta.astype(BF16))
    vmt_pad = jnp.transpose(vm_pad.reshape(LANES, N_HEADS, V_DIM), (1, 2, 0))
    vec_spec3 = lambda w: pl.BlockSpec((1, w), lambda b, h, i: (0, 0))
    oatt_p = pl.pallas_call(
        _attn_prompt_kernel,
        grid=(B, N_HEADS, SEQ // tq),
        in_specs=[
            pl.BlockSpec((None, tq, V_DIM), lambda b, h, i: (b, i, h)),
            pl.BlockSpec((None, SEQ, V_DIM), lambda b, h, i: (b, 0, h)),
            pl.BlockSpec((None, None, SEQ // tq, V_DIM, tq), lambda b, h, i: (b, h, 0, 0, 0)),
            pl.BlockSpec((LANES, V_DIM), lambda b, h, i: (0, h)),
            pl.BlockSpec((None, V_DIM, LANES), lambda b, h, i: (h, 0, 0)),
            vec_spec3(HEAD_DIM), vec_spec3(HEAD_DIM), vec_spec3(HEAD_DIM), vec_spec3(HEAD_DIM),
            pl.BlockSpec((V_DIM, 1), lambda b, h, i: (0, 0)),
        ],
        out_specs=pl.BlockSpec((None, tq, V_DIM), lambda b, h, i: (b, i, h)),
        out_shape=jax.ShapeDtypeStruct((B, SEQ, D_ATTN), BF16),
        scratch_shapes=[pltpu.VMEM((2 * tq, V_DIM), BF16), pltpu.VMEM((1, 2 * tq), F32), pltpu.VMEM((1, 2 * tq), F32),
                        pltpu.VMEM((V_DIM, 2 * tq), F32), pltpu.VMEM((tq, 2 * tq), F32), pltpu.VMEM((tq, 2 * tq), F32)],
        compiler_params=_params("parallel", "parallel", "arbitrary"),
        name="attn_prompt",
    )(q_p, kb_p, vt_p, km_pad, vmt_pad, lq1, lk1, lq2, lk2, sg.reshape(V_DIM, 1))

    hs = N_HEADS * S_dec
    qd = q_s[:n_s].reshape(S_dec, Bd, N_HEADS, 2, HEAD_DIM)
    qd = jnp.transpose(qd, (1, 3, 2, 0, 4))
    qmat = jnp.zeros((Bd, 2, hs, 2, HEAD_DIM), F32)
    qmat = qmat.at[:, 0, :, 0].set(qd[:, 0].reshape(Bd, hs, HEAD_DIM)).at[:, 1, :, 1].set(qd[:, 1].reshape(Bd, hs, HEAD_DIM))
    qmat = qmat.reshape(Bd, 2 * hs, V_DIM).astype(BF16)
    new_rows = lambda a: jnp.transpose(a[:n_s].reshape(S_dec, Bd, hs // S_dec * V_DIM), (1, 0, 2)).reshape(Bd, hs, V_DIM)
    kn_pad = jnp.zeros((Bd, LANES, V_DIM), BF16).at[:, :hs].set(new_rows(k_s).astype(BF16))
    vn_pad = jnp.zeros((Bd, LANES, V_DIM), BF16).at[:, :hs].set(new_rows(v_s).astype(BF16))
    ck = cache_k.reshape(n_phys, page * N_HEADS, V_DIM)
    cv = cache_v.reshape(n_phys, page * N_HEADS, V_DIM)
    n_pg = math.gcd(DECODE_PAGES, n_pages)
    vec_spec2 = lambda w: pl.BlockSpec((1, w), lambda b, g, pt: (0, 0))
    per_b = lambda r: pl.BlockSpec((None, r, V_DIM), lambda b, g, pt: (b, 0, 0))
    hbm_spec = pl.BlockSpec(memory_space=pl.ANY)
    page_buf = pltpu.VMEM((DECODE_SLOTS, n_pg, page * N_HEADS, V_DIM), F32)
    oatt_s = pl.pallas_call(
        functools.partial(_decode_kernel, n_pg=n_pg, s_dec=S_dec),
        grid_spec=pltpu.PrefetchScalarGridSpec(
            num_scalar_prefetch=1,
            grid=(Bd, n_pages // n_pg),
            in_specs=[per_b(2 * hs), per_b(LANES), per_b(LANES)] + [vec_spec2(HEAD_DIM)] * 4 + [vec_spec2(V_DIM)]
            + [hbm_spec, hbm_spec],
            out_specs=per_b(hs),
            scratch_shapes=[page_buf, page_buf, pltpu.SemaphoreType.DMA((DECODE_SLOTS, 2)),
                            pltpu.VMEM((2 * hs, 1), F32), pltpu.VMEM((2 * hs, 1), F32), pltpu.VMEM((2 * hs, V_DIM), F32)],
        ),
        out_shape=jax.ShapeDtypeStruct((Bd, hs, V_DIM), BF16),
        compiler_params=_params("arbitrary", "arbitrary"),
        name="attn_decode",
    )(page_table, qmat, kn_pad, vn_pad, lq1, lk1, lq2, lk2, sg, ck, cv)
    oatt_s = jnp.transpose(oatt_s.reshape(Bd, N_HEADS, S_dec, V_DIM), (2, 0, 1, 3)).reshape(n_s, D_ATTN)

    y_p = _routed_moe(x_prompt.reshape(B * SEQ, D), oatt_p.reshape(B * SEQ, D_ATTN), opool_p.reshape(B * SEQ, D_POOL),
                      wo, nfg, rw, rb, wg, wu, wd)
    y_s = _moe_call(xs_t, oatt_s, opool_s, wo, nfg, rw, rb, wg, wu, wd, min(MOE_ROWS, n_s))

    y_prompt = y_p.reshape(B, SEQ, D)
    y_sample = jnp.transpose(y_s.reshape(S_dec, Bd, D), (1, 0, 2))
    bc = lambda a: jnp.broadcast_to(a[None], (B,) + a.shape)
    heads = lambda a: bc(a.reshape(N_META, N_HEADS, V_DIM))
    new_k_prompt = jnp.concatenate([heads(k_meta), kf_p], axis=1)[None]
    new_v_prompt = jnp.concatenate([heads(v_meta), vf_p], axis=1)[None]
    new_pool_prompt = utail_p[:, MAX_WINDOW - POOL_STATE:][None]
    to_b = lambda a, w: jnp.transpose(a[:n_s].reshape(S_dec, Bd, w), (1, 0, 2))
    new_k_sample = to_b(k_s, D_QK).reshape(1, Bd, S_dec, N_HEADS, V_DIM)
    new_v_sample = to_b(v_s, D_ATTN).reshape(1, Bd, S_dec, N_HEADS, V_DIM)
    u_new = to_b(u_s, D_POOL)
    new_pool_sample = jnp.concatenate([state_pool[0].astype(F32), u_new], axis=1)[:, S_dec:][None]
    return (y_prompt, y_sample, new_k_prompt, new_v_prompt, new_pool_prompt, new_k_sample, new_v_sample,
            new_pool_sample)
```
